```python
import functools
import jax, jax.numpy as jnp
from jax import lax
import numpy as np

D_MODEL = 1024
BATCH = 4
SEQ = 4096
DEPTH = 4
DEC_BATCH = 128
DEC_SEQ = 1
PAST_LEN = 2048
PAGE_SIZE = 128

N_HEADS = 8
HEAD_DIM = 64
ATT_W = N_HEADS * HEAD_DIM
IDX_HEADS = 4
IDX_DIM = 64
TOPK_MAX = 256
Q_BLOCK = 128
ROPE_THETA = 10000.0
RET_HEADS = 4
RET_DK = 64
RET_DV = 128
RET_QK = RET_HEADS * RET_DK
RET_V = RET_HEADS * RET_DV
RET_CHUNK = 128
CONV_CH = 512
CONV_W = 31
N_EXPERTS = 32
MOE_TOP_K = 4
D_EXPERT = 1024
SWIGLU_LIMIT = 7.0
SWIGLU_ALPHA = 1.702
MOE_BLOCK = 128
N_BRANCH = 3
ALPHA = (2 * DEPTH) ** 0.25
BETA = (8 * DEPTH) ** -0.25
LN_EPS = 1e-5
IN_SPLITS = (ATT_W, ATT_W, ATT_W, IDX_HEADS * IDX_DIM, IDX_DIM, IDX_HEADS, RET_QK, RET_QK, RET_V, RET_V, 2 * CONV_CH, N_BRANCH * D_MODEL)
IN_WIDTH = sum(IN_SPLITS)
IN_OFFSETS = tuple(int(o) for o in np.cumsum(IN_SPLITS)[:-1])

kernel_name = 'hybrid_retention_conformer_dsa_moe_step'

F32 = jnp.float32


def layer_norm(x, g, b):
    xf = x.astype(F32)
    mu = xf.mean(-1, keepdims=True)
    var = ((xf - mu) ** 2).mean(-1, keepdims=True)
    return ((xf - mu) * lax.rsqrt(var + LN_EPS)).astype(x.dtype) * g + b


def head_norm(x):
    xf = x.astype(F32)
    mu = xf.mean(-1, keepdims=True)
    var = ((xf - mu) ** 2).mean(-1, keepdims=True)
    return ((xf - mu) * lax.rsqrt(var + LN_EPS)).astype(x.dtype)


def rope(x, pos):
    half = x.shape[-1] // 2
    inv = 1.0 / (ROPE_THETA ** (jnp.arange(half, dtype=F32) / half))
    ang = pos.astype(F32)[:, None] * inv[None, :]
    cos = jnp.cos(ang)[None, :, None, :]
    sin = jnp.sin(ang)[None, :, None, :]
    xf = x.astype(F32)
    x1, x2 = xf[..., :half], xf[..., half:]
    return jnp.concatenate([x1 * cos - x2 * sin, x2 * cos + x1 * sin], axis=-1).astype(x.dtype)


def in_projection(u, w_in, pos):
    B, L, _ = u.shape
    aq, ak, av, iq, ik, iw, rq, rk, rv, rg, cglu, gl = jnp.split(u @ w_in, IN_OFFSETS, axis=-1)
    aq = rope(aq.reshape(B, L, N_HEADS, HEAD_DIM), pos)
    ak = rope(ak.reshape(B, L, N_HEADS, HEAD_DIM), pos)
    av = av.reshape(B, L, N_HEADS, HEAD_DIM)
    iq = rope(iq.reshape(B, L, IDX_HEADS, IDX_DIM), pos)
    ik = rope(ik.reshape(B, L, 1, IDX_DIM), pos).reshape(B, L, IDX_DIM)
    rq = rope(rq.reshape(B, L, RET_HEADS, RET_DK), pos)
    rk = rope(rk.reshape(B, L, RET_HEADS, RET_DK), pos) * (RET_DK ** -0.5)
    rv = rv.reshape(B, L, RET_HEADS, RET_DV)
    return aq, ak, av, iq, ik, iw, rq, rk, rv, rg, cglu, gl


def retention_log_decay():
    return jnp.log(1.0 - 2.0 ** (-5.0 - jnp.arange(RET_HEADS, dtype=F32)))


def retention_chunk(q, k, v, s_prev):
    C = q.shape[1]
    lg = retention_log_decay()
    i = jnp.arange(C, dtype=F32)
    diff = i[:, None] - i[None, :]
    causal = diff >= 0
    decay = jnp.where(causal[None], jnp.exp(jnp.where(causal, diff, 0.0)[None] * lg[:, None, None]), 0.0)
    scores = jnp.einsum('bihd,bjhd->bhij', q, k) * decay[None]
    o = jnp.einsum('bhij,bjhe->bihe', scores, v)
    o = o + jnp.einsum('bihd,bhde->bihe', q, s_prev) * jnp.exp((i[:, None] + 1.0) * lg[None, :])[None, :, :, None]
    w_k = jnp.exp((C - 1.0 - i)[:, None] * lg[None, :])
    s_new = jnp.exp(C * lg)[None, :, None, None] * s_prev + jnp.einsum('bjhd,bjhe->bhde', k * w_k[None, :, :, None], v)
    return o, s_new


def prompt_retention(q, k, v):
    B, L, H, dk = q.shape
    dv = v.shape[-1]
    nc = L // RET_CHUNK

    def to_chunks(t):
        return jnp.moveaxis(t.astype(F32).reshape(B, nc, RET_CHUNK, *t.shape[2:]), 1, 0)

    def step(s, qkv):
        o, s = retention_chunk(qkv[0], qkv[1], qkv[2], s)
        return s, o

    s0 = jnp.zeros((B, H, dk, dv), F32)
    s_fin, o = lax.scan(step, s0, (to_chunks(q), to_chunks(k), to_chunks(v)))
    o = jnp.moveaxis(o, 0, 1).reshape(B, L, H, dv)
    return o.astype(q.dtype), s_fin.astype(q.dtype)


def conformer_conv(glu_in, buf, dw, db, ln_g, ln_b, w_out, b_out):
    a, gate = jnp.split(glu_in, 2, axis=-1)
    z = a * jax.nn.sigmoid(gate)
    zp = jnp.concatenate([buf.astype(z.dtype), z], axis=1)
    y = lax.conv_general_dilated(zp, dw[:, None, :].astype(z.dtype), (1,), 'VALID',
                                 dimension_numbers=('NWC', 'WIO', 'NWC'), feature_group_count=CONV_CH) + db
    y = jax.nn.silu(layer_norm(y, ln_g, ln_b))
    return y @ w_out + b_out, zp[:, -(CONV_W - 1):]


take_rows = jax.vmap(lambda rows, idx: rows[idx])


def index_scores(iq, iw, ik):
    s = jnp.einsum('bthd,bsd->bths', iq.astype(F32), ik.astype(F32)) * (IDX_DIM ** -0.5)
    return jnp.einsum('bth,bths->bts', iw.astype(F32) * (IDX_HEADS ** -0.5), jax.nn.relu(s))


def select_keys(scores, qpos, topk):
    kpos = jnp.arange(scores.shape[-1], dtype=jnp.int32)
    admissible = kpos[None, None, :] <= qpos[None, :, None]
    _, idx = lax.top_k(jnp.where(admissible, scores, -jnp.inf), topk)
    return idx, idx <= qpos[None, :, None]


def attend_selected(q, k_sel, v_sel, valid):
    s = jnp.einsum('bthd,btkhd->bthk', q, k_sel).astype(F32) * (HEAD_DIM ** -0.5)
    s = jnp.where(valid[:, :, None, :], s, -jnp.inf)
    p = jax.nn.softmax(s, axis=-1).astype(v_sel.dtype)
    o = jnp.einsum('bthk,btkhd->bthd', p, v_sel)
    return o.reshape(o.shape[0], o.shape[1], -1)


def prompt_sparse_attention(q, k, v, iq, ik, iw):
    B, L = q.shape[:2]
    nb = L // Q_BLOCK
    topk = min(TOPK_MAX, L // 4)

    def blocks(t):
        return jnp.moveaxis(t.reshape(B, nb, Q_BLOCK, *t.shape[2:]), 1, 0)

    def one_block(args):
        t0, qb, iqb, iwb = args
        qpos = t0 + jnp.arange(Q_BLOCK, dtype=jnp.int32)
        idx, valid = select_keys(index_scores(iqb, iwb, ik), qpos, topk)
        return attend_selected(qb, take_rows(k, idx), take_rows(v, idx), valid)

    o = lax.map(one_block, (jnp.arange(nb, dtype=jnp.int32) * Q_BLOCK, blocks(q), blocks(iq), blocks(iw)))
    return jnp.moveaxis(o, 0, 1).reshape(B, L, ATT_W)


def sample_sparse_attention(q, k, v, iq, ik, iw, pool_k, pool_v, pool_ik, page_table):
    DB, T = q.shape[:2]
    past = page_table.shape[1] * PAGE_SIZE
    ik_past = pool_ik[page_table].reshape(DB, past, IDX_DIM)
    ik_all = jnp.concatenate([ik_past, ik.astype(ik_past.dtype)], axis=1)
    qpos = past + jnp.arange(T, dtype=jnp.int32)
    idx, valid = select_keys(index_scores(iq, iw, ik_all), qpos, min(TOPK_MAX, (past + T) // 4))
    in_past = (idx < past)[..., None, None]
    pidx = jnp.minimum(idx, past - 1)
    phys = jax.vmap(lambda pt, i: pt[i])(page_table, pidx // PAGE_SIZE)
    off = pidx % PAGE_SIZE
    nidx = jnp.clip(idx - past, 0, T - 1)
    k_sel = jnp.where(in_past, pool_k[phys, off].astype(k.dtype), take_rows(k, nidx))
    v_sel = jnp.where(in_past, pool_v[phys, off].astype(v.dtype), take_rows(v, nidx))
    return attend_selected(q, k_sel, v_sel, valid)


def mixer_merge(gl, o_ret, rg, conv_y, att, w_ret_out, w_att_out, w_o):
    y_ret = (jax.nn.silu(rg) * head_norm(o_ret).reshape(rg.shape)) @ w_ret_out
    y_att = att @ w_att_out
    g_r, g_c, g_a = jnp.split(jax.nn.sigmoid(gl), N_BRANCH, axis=-1)
    return (g_r * y_ret + g_c * conv_y + g_a * y_att) @ w_o


def prompt_mixer(u, mw):
    w_in, w_ret_out, conv_dw, conv_db, conv_ln_g, conv_ln_b, w_conv_out, b_conv_out, w_att_out, w_o = mw
    B, L, _ = u.shape
    pos = jnp.arange(L, dtype=jnp.int32)
    aq, ak, av, iq, ik, iw, rq, rk, rv, rg, cglu, gl = in_projection(u, w_in, pos)
    att = prompt_sparse_attention(aq, ak, av, iq, ik, iw)
    o_ret, s_ret = prompt_retention(rq, rk, rv)
    buf0 = jnp.zeros((B, CONV_W - 1, CONV_CH), u.dtype)
    conv_y, buf = conformer_conv(cglu, buf0, conv_dw, conv_db, conv_ln_g, conv_ln_b, w_conv_out, b_conv_out)
    mix = mixer_merge(gl, o_ret, rg, conv_y, att, w_ret_out, w_att_out, w_o)
    return mix, (ak, av, ik, s_ret, buf)


def sample_mixer(u, pool_k, pool_v, pool_ik, s_ret_prev, conv_buf, page_table, mw):
    w_in, w_ret_out, conv_dw, conv_db, conv_ln_g, conv_ln_b, w_conv_out, b_conv_out, w_att_out, w_o = mw
    T = u.shape[1]
    pos = page_table.shape[1] * PAGE_SIZE + jnp.arange(T, dtype=jnp.int32)
    aq, ak, av, iq, ik, iw, rq, rk, rv, rg, cglu, gl = in_projection(u, w_in, pos)
    att = sample_sparse_attention(aq, ak, av, iq, ik, iw, pool_k, pool_v, pool_ik, page_table)
    o_ret, s_ret = retention_chunk(rq.astype(F32), rk.astype(F32), rv.astype(F32), s_ret_prev.astype(F32))
    conv_y, buf = conformer_conv(cglu, conv_buf, conv_dw, conv_db, conv_ln_g, conv_ln_b, w_conv_out, b_conv_out)
    mix = mixer_merge(gl, o_ret.astype(u.dtype), rg, conv_y, att, w_ret_out, w_att_out, w_o)
    return mix, (ak, av, ik, s_ret.astype(s_ret_prev.dtype), buf)


def moe_ffn(h, router_w, router_b, w_gu, b_gu, w_down, b_down):
    B, L, D = h.shape
    xt = h.reshape(-1, D)
    n_tok = xt.shape[0]
    n_assign = n_tok * MOE_TOP_K
    logits = (xt @ router_w + router_b).astype(F32)
    top_val, top_e = lax.top_k(logits, MOE_TOP_K)
    gate = jax.nn.softmax(top_val, axis=-1)
    flat_e = top_e.reshape(-1)
    flat_tok = jnp.arange(n_assign, dtype=jnp.int32) // MOE_TOP_K
    order = jnp.argsort(flat_e)
    se, stok, sg = flat_e[order], flat_tok[order], gate.reshape(-1)[order]
    counts = jnp.bincount(flat_e, length=N_EXPERTS)
    starts = jnp.cumsum(counts) - counts
    rank = jnp.arange(n_assign, dtype=jnp.int32) - starts[se]
    pcounts = (counts + MOE_BLOCK - 1) // MOE_BLOCK * MOE_BLOCK
    pends = jnp.cumsum(pcounts)
    ppos = (pends - pcounts)[se] + rank
    n_blocks = -(-n_assign // MOE_BLOCK) + N_EXPERTS
    n_slots = n_blocks * MOE_BLOCK
    buf_tok = jnp.zeros((n_slots,), jnp.int32).at[ppos].set(stok)
    buf_g = jnp.zeros((n_slots,), F32).at[ppos].set(sg)
    blk_e = jnp.minimum(jnp.searchsorted(pends, jnp.arange(n_blocks) * MOE_BLOCK, side='right'), N_EXPERTS - 1)

    def run_block(args):
        e, tok, g = args
        gu = xt[tok] @ w_gu[e] + b_gu[e]
        a, lin = jnp.split(gu, 2, axis=-1)
        a = jnp.minimum(a, SWIGLU_LIMIT)
        lin = jnp.clip(lin, -SWIGLU_LIMIT, SWIGLU_LIMIT)
        hid = a * jax.nn.sigmoid(SWIGLU_ALPHA * a) * (lin + 1.0)
        return (hid @ w_down[e] + b_down[e]) * g[:, None].astype(xt.dtype)

    outs = lax.map(run_block, (blk_e, buf_tok.reshape(n_blocks, MOE_BLOCK), buf_g.reshape(n_blocks, MOE_BLOCK)))
    y = jnp.zeros((n_tok, D), xt.dtype).at[buf_tok].add(outs.reshape(n_slots, D))
    return y.reshape(B, L, D)


def apply_layer(x, cvec, mixer_fn, ada_w, ada_b, ln1_g, ln1_b, ffn_w, ln2_g, ln2_b):
    sh1, sc1, g1, sh2, sc2, g2 = jnp.split((jax.nn.silu(cvec) @ ada_w + ada_b)[:, None, :], 6, axis=-1)
    mix, new_state = mixer_fn(x * (1.0 + sc1) + sh1)
    x = layer_norm(ALPHA * x + g1 * mix, ln1_g, ln1_b)
    ff = moe_ffn(x * (1.0 + sc2) + sh2, *ffn_w)
    x = layer_norm(ALPHA * x + g2 * ff, ln2_g, ln2_b)
    return x, new_state


def setup_inputs(seed: int = 0) -> dict:
    key = jax.random.key(seed)
    ks = iter(jax.random.split(key, 48))
    n_pages = PAST_LEN // PAGE_SIZE
    n_used = DEC_BATCH * n_pages
    n_phys = n_used + max(1, n_used // 4)
    D = D_MODEL

    def nrm(shape, scale=1.0):
        return jax.random.normal(next(ks), shape, F32) * scale

    page_table = jax.random.permutation(next(ks), n_phys)[:n_used].reshape(DEC_BATCH, n_pages).astype(jnp.int32)
    return {
        'x_prompt': nrm((BATCH, SEQ, D)),
        'x_sample': nrm((DEC_BATCH, DEC_SEQ, D)),
        'c_prompt': nrm((BATCH, D)),
        'c_sample': nrm((DEC_BATCH, D)),
        'cache_k': nrm((DEPTH, n_phys, PAGE_SIZE, N_HEADS, HEAD_DIM)),
        'cache_v': nrm((DEPTH, n_phys, PAGE_SIZE, N_HEADS, HEAD_DIM)),
        'cache_idx_k': nrm((DEPTH, n_phys, PAGE_SIZE, IDX_DIM)),
        'state_ret': nrm((DEPTH, DEC_BATCH, RET_HEADS, RET_DK, RET_DV), 0.3),
        'state_conv': nrm((DEPTH, DEC_BATCH, CONV_W - 1, CONV_CH), 0.5),
        'page_table': page_table,
        'ada_w': nrm((DEPTH, D, 6 * D), D ** -0.5),
        'ada_b': nrm((DEPTH, 6 * D), 0.02),
        'w_in': nrm((DEPTH, D, IN_WIDTH), D ** -0.5),
        'w_ret_out': nrm((DEPTH, RET_V, D), RET_V ** -0.5),
        'conv_dw': nrm((DEPTH, CONV_W, CONV_CH), CONV_W ** -0.5),
        'conv_db': nrm((DEPTH, CONV_CH), 0.02),
        'conv_ln_g': 1.0 + nrm((DEPTH, CONV_CH), 0.02),
        'conv_ln_b': nrm((DEPTH, CONV_CH), 0.02),
        'w_conv_out': nrm((DEPTH, CONV_CH, D), CONV_CH ** -0.5),
        'b_conv_out': nrm((DEPTH, D), 0.02),
        'w_att_out': nrm((DEPTH, ATT_W, D), ATT_W ** -0.5),
        'w_o': nrm((DEPTH, D, D), BETA * D ** -0.5),
        'ln1_g': 1.0 + nrm((DEPTH, D), 0.02),
        'ln1_b': nrm((DEPTH, D), 0.02),
        'router_w': nrm((DEPTH, D, N_EXPERTS), D ** -0.5),
        'router_b': nrm((DEPTH, N_EXPERTS), 0.01),
        'exp_w_gu': nrm((DEPTH, N_EXPERTS, D, 2 * D_EXPERT), D ** -0.5),
        'exp_b_gu': nrm((DEPTH, N_EXPERTS, 2 * D_EXPERT), 0.02),
        'exp_w_down': nrm((DEPTH, N_EXPERTS, D_EXPERT, D), BETA * D_EXPERT ** -0.5),
        'exp_b_down': nrm((DEPTH, N_EXPERTS, D), 0.02),
        'ln2_g': 1.0 + nrm((DEPTH, D), 0.02),
        'ln2_b': nrm((DEPTH, D), 0.02),
    }


def reference(x_prompt, x_sample, c_prompt, c_sample, cache_k, cache_v, cache_idx_k, state_ret, state_conv, page_table,
              ada_w, ada_b, w_in, w_ret_out, conv_dw, conv_db, conv_ln_g, conv_ln_b, w_conv_out, b_conv_out,
              w_att_out, w_o, ln1_g, ln1_b, router_w, router_b, exp_w_gu, exp_b_gu, exp_w_down, exp_b_down,
              ln2_g, ln2_b):
    xp, xs = x_prompt, x_sample
    new_p = ([], [], [], [], [])
    new_s = ([], [], [], [], [])
    for l in range(DEPTH):
        mw = (w_in[l], w_ret_out[l], conv_dw[l], conv_db[l], conv_ln_g[l], conv_ln_b[l],
              w_conv_out[l], b_conv_out[l], w_att_out[l], w_o[l])
        ffn_w = (router_w[l], router_b[l], exp_w_gu[l], exp_b_gu[l], exp_w_down[l], exp_b_down[l])
        xp, st_p = apply_layer(xp, c_prompt, functools.partial(prompt_mixer, mw=mw),
                               ada_w[l], ada_b[l], ln1_g[l], ln1_b[l], ffn_w, ln2_g[l], ln2_b[l])
        sample_fn = functools.partial(sample_mixer, pool_k=cache_k[l], pool_v=cache_v[l], pool_ik=cache_idx_k[l],
                                      s_ret_prev=state_ret[l], conv_buf=state_conv[l], page_table=page_table, mw=mw)
        xs, st_s = apply_layer(xs, c_sample, sample_fn,
                               ada_w[l], ada_b[l], ln1_g[l], ln1_b[l], ffn_w, ln2_g[l], ln2_b[l])
        for lst, a in zip(new_p, st_p):
            lst.append(a)
        for lst, a in zip(new_s, st_s):
            lst.append(a)
    nk_p, nv_p, nik_p, nr_p, nc_p = [jnp.stack(a) for a in new_p]
    nk_s, nv_s, nik_s, nr_s, nc_s = [jnp.stack(a) for a in new_s]
    return (xp, xs, nk_p, nv_p, nik_p, nr_p, nc_p, nk_s, nv_s, nik_s, nr_s, nc_s)
```

```python
import functools
import math

import numpy as np
import jax
import jax.numpy as jnp
from jax import lax
from jax.experimental import pallas as pl
from jax.experimental.pallas import tpu as pltpu

F32 = jnp.float32
BF16 = jnp.bfloat16
I32 = jnp.int32

D_MODEL = 1024
DEPTH = 4
PAGE_SIZE = 128
N_HEADS = 8
HEAD_DIM = 64
ATT_W = N_HEADS * HEAD_DIM
IDX_HEADS = 4
IDX_DIM = 64
TOPK_MAX = 256
Q_BLOCK = 128
ROPE_THETA = 10000.0
RET_HEADS = 4
RET_DK = 64
RET_DV = 128
RET_QK = RET_HEADS * RET_DK
RET_V = RET_HEADS * RET_DV
RET_CHUNK = 128
CONV_CH = 512
CONV_W = 31
N_EXPERTS = 32
MOE_TOP_K = 4
D_EXPERT = 1024
SWIGLU_LIMIT = 7.0
SWIGLU_ALPHA = 1.702
N_BRANCH = 3
ALPHA = (2 * DEPTH) ** 0.25
LN_EPS = 1e-5
IN_SPLITS = (ATT_W, ATT_W, ATT_W, IDX_HEADS * IDX_DIM, IDX_DIM, IDX_HEADS, RET_QK, RET_QK, RET_V, RET_V,
             2 * CONV_CH, N_BRANCH * D_MODEL)
IN_OFFSETS = tuple(int(o) for o in np.cumsum((0,) + IN_SPLITS))

LANES = 128
INT_MIN = -2 ** 31
NEG_BIG = -1e30
MOE_BM = 256
KEY_CHUNK = 512
VMEM_LIMIT = 56 * 1024 * 1024


def _dot(a, b):
    return jnp.dot(a, b, preferred_element_type=F32)


def _dot_nt(a, b):
    return lax.dot_general(a, b, (((1,), (1,)), ((), ())), preferred_element_type=F32)


def _dot_tn(a, b):
    return lax.dot_general(a, b, (((0,), (0,)), ((), ())), preferred_element_type=F32)


def _silu(x):
    return x * jax.nn.sigmoid(x)


def _norm_rows(x):
    mu = jnp.mean(x, axis=-1, keepdims=True)
    xc = x - mu
    var = jnp.mean(xc * xc, axis=-1, keepdims=True)
    return xc * lax.rsqrt(var + LN_EPS)


def _params(*sem):
    return pltpu.CompilerParams(dimension_semantics=sem, vmem_limit_bytes=VMEM_LIMIT)


def _ada_kernel(c_ref, w_ref, b_ref, o_ref):
    c = c_ref[...]
    o_ref[0] = _dot(_silu(c).astype(BF16), w_ref[0].astype(BF16)) + b_ref[0]


def ada_modulation(c_all, ada_w, ada_b):
    R, D = c_all.shape
    W = ada_w.shape[-1]
    tn = 1536
    return pl.pallas_call(
        _ada_kernel,
        out_shape=jax.ShapeDtypeStruct((DEPTH, R, W), F32),
        grid=(DEPTH, W // tn),
        in_specs=[pl.BlockSpec((R, D), lambda l, j: (0, 0)),
                  pl.BlockSpec((1, D, tn), lambda l, j: (l, 0, j)),
                  pl.BlockSpec((1, 1, tn), lambda l, j: (l, 0, j))],
        out_specs=pl.BlockSpec((1, R, tn), lambda l, j: (l, 0, j)),
        compiler_params=_params("arbitrary", "arbitrary"),
        name="ada_modulation",
    )(c_all, ada_w, ada_b.reshape(DEPTH, 1, W))


def _rope_chunks(y, cos, sin, first_half):
    parts = []
    for c in range(y.shape[1] // LANES):
        yc = y[:, c * LANES:(c + 1) * LANES]
        rot = jnp.where(first_half, pltpu.roll(yc, LANES - HEAD_DIM // 2, 1), pltpu.roll(yc, HEAD_DIM // 2, 1))
        parts.append(yc * cos + rot * sin)
    return parts[0] if len(parts) == 1 else jnp.concatenate(parts, axis=1)


_R_AQ, _R_AK, _R_IQ, _R_RQ, _R_RK, _R_IK, _R_END = 0, 512, 1024, 1280, 1536, 1792, 1920


def _proj_rope_kernel(x_ref, sc_ref, sh_ref, cos_ref, sin_ref, w_ref,
                      aq_ref, ak_ref, akb_ref, iq_ref, rq_ref, rk_ref, ik_ref, ikb_ref):
    u = (x_ref[0] * (1.0 + sc_ref[0]) + sh_ref[0]).astype(BF16)
    cos = cos_ref[...]
    sin = sin_ref[...]
    lane = lax.broadcasted_iota(I32, cos.shape, 1)
    first_half = (lane % HEAD_DIM) < (HEAD_DIM // 2)

    def proj(lo, hi):
        return _rope_chunks(_dot(u, w_ref[:, lo:hi]), cos, sin, first_half)

    aq_ref[0] = proj(_R_AQ, _R_AK).astype(BF16)
    ak = proj(_R_AK, _R_IQ)
    ak_ref[0] = ak
    akb_ref[0] = ak.astype(BF16)
    iq_ref[0] = proj(_R_IQ, _R_RQ).astype(BF16)
    rq_ref[0] = proj(_R_RQ, _R_RK).astype(BF16)
    rk_ref[0] = proj(_R_RK, _R_IK)
    ik = proj(_R_IK, _R_END)[:, :IDX_DIM]
    ik_ref[0] = ik
    ikb_ref[0] = ik.astype(BF16)


_P_AV, _P_RV, _P_RG, _P_CA, _P_CG, _P_IW, _P_END = 0, 512, 1024, 1536, 2048, 2560, 2688


def _proj_plain_kernel(x_ref, sc_ref, sh_ref, w_ref, av_ref, avb_ref, rv_ref, rg_ref, z_ref, iw_ref):
    u = (x_ref[0] * (1.0 + sc_ref[0]) + sh_ref[0]).astype(BF16)
    av = _dot(u, w_ref[:, _P_AV:_P_RV])
    av_ref[0] = av
    avb_ref[0] = av.astype(BF16)
    rv_ref[0] = _dot(u, w_ref[:, _P_RV:_P_RG]).astype(BF16)
    rg_ref[0] = _dot(u, w_ref[:, _P_RG:_P_CA])
    a = _dot(u, w_ref[:, _P_CA:_P_CG])
    g = _dot(u, w_ref[:, _P_CG:_P_IW])
    z_ref[0] = a * jax.nn.sigmoid(g)
    iw_ref[0] = _dot(u, w_ref[:, _P_IW:_P_END])


def _proj_gate_kernel(x_ref, sc_ref, sh_ref, w_ref, gs_ref):
    u = (x_ref[0] * (1.0 + sc_ref[0]) + sh_ref[0]).astype(BF16)
    for c in range(N_BRANCH):
        gs_ref[0, :, c * D_MODEL:(c + 1) * D_MODEL] = jax.nn.sigmoid(
            _dot(u, w_ref[:, c * D_MODEL:(c + 1) * D_MODEL]))


def _mod_spec(mod, tm):
    if mod.shape[1] == 1:
        return pl.BlockSpec((1, 1, D_MODEL), lambda b, i: (b, 0, 0))
    return pl.BlockSpec((1, tm, D_MODEL), lambda b, i: (b, i, 0))


def input_projection(x, sc, sh, cos_t, sin_t, w_rope, w_plain, w_gate, tm):
    B, L, D = x.shape
    grid = (B, L // tm)
    xs = pl.BlockSpec((1, tm, D), lambda b, i: (b, i, 0))
    ms = _mod_spec(sc, tm)
    ts = pl.BlockSpec((tm, LANES), lambda b, i: (i, 0))

    def ws(w):
        return pl.BlockSpec(w.shape, lambda b, i: (0, 0))

    def o(width, dt):
        return jax.ShapeDtypeStruct((B, L, width), dt), pl.BlockSpec((1, tm, width), lambda b, i: (b, i, 0))

    outs = [o(ATT_W, BF16), o(ATT_W, F32), o(ATT_W, BF16), o(IDX_HEADS * IDX_DIM, BF16), o(RET_QK, BF16),
            o(RET_QK, F32), o(IDX_DIM, F32), o(IDX_DIM, BF16)]
    aq, ak, akb, iq, rq, rk, ik, ikb = pl.pallas_call(
        _proj_rope_kernel,
        out_shape=[s for s, _ in outs], grid=grid,
        in_specs=[xs, ms, ms, ts, ts, ws(w_rope)],
        out_specs=[s for _, s in outs],
        compiler_params=_params("arbitrary", "arbitrary"), name="proj_rope",
    )(x, sc, sh, cos_t, sin_t, w_rope)

    outs = [o(ATT_W, F32), o(ATT_W, BF16), o(RET_V, BF16), o(RET_V, F32), o(CONV_CH, F32), o(LANES, F32)]
    av, avb, rv, rg, z, iw = pl.pallas_call(
        _proj_plain_kernel,
        out_shape=[s for s, _ in outs], grid=grid,
        in_specs=[xs, ms, ms, ws(w_plain)],
        out_specs=[s for _, s in outs],
        compiler_params=_params("arbitrary", "arbitrary"), name="proj_plain",
    )(x, sc, sh, w_plain)

    gs_shape, gs_spec = o(N_BRANCH * D_MODEL, F32)
    gsig = pl.pallas_call(
        _proj_gate_kernel,
        out_shape=gs_shape, grid=grid,
        in_specs=[xs, ms, ms, ws(w_gate)],
        out_specs=gs_spec,
        compiler_params=_params("arbitrary", "arbitrary"), name="proj_gate",
    )(x, sc, sh, w_gate)
    return dict(aq=aq, ak=ak, akb=akb, iq=iq, rq=rq, rk=rk, ik=ik, ikb=ikb,
                av=av, avb=avb, rv=rv, rg=rg, z=z, iw=iw, gsig=gsig)


def _sort_key(score):
    score = jnp.where(score == 0.0, 0.0, score)
    b = pltpu.bitcast(score, I32)
    return b ^ ((b >> 31) & 0x7FFFFFFF)


def _select_threshold(count_ge, count_tie_below, rows, topk, pos_bits):
    k = float(topk)
    t = jnp.where(count_ge(jnp.zeros((rows, 1), I32)) >= k, 0, INT_MIN).astype(I32)

    def bit_step(i, t):
        cand = t | jnp.left_shift(1, 30 - i).astype(I32)
        return jnp.where(count_ge(cand) >= k, cand, t)

    t = lax.fori_loop(0, 31, bit_step, t)
    need = k - count_ge(t + 1)

    def pos_step(i, j):
        cand = j | jnp.left_shift(1, pos_bits - 1 - i).astype(I32)
        return jnp.where(count_tie_below(t, cand) < need, cand, j)

    j = lax.fori_loop(0, pos_bits, pos_step, jnp.zeros((rows, 1), I32))
    return t, j


def _dsa_prompt_kernel(q_ref, iq_ref, iw_ref, kb_ref, vb_ref, ikb_ref, o_ref,
                       key_scr, bias_scr, m_scr, l_scr, acc_scr, *, topk, ck, pos_bits):
    qb = q_ref.shape[1]
    t0 = pl.program_id(1) * qb
    nkc = (t0 + qb + ck - 1) // ck
    qpos = t0 + lax.broadcasted_iota(I32, (qb, 1), 0)
    iw = iw_ref[0]

    def chunk_pos(c):
        off = pl.multiple_of(c * ck, ck)
        return off, off + lax.broadcasted_iota(I32, (qb, ck), 1)

    def score_chunk(c, carry):
        off, kpos = chunk_pos(c)
        ikc = ikb_ref[0, pl.ds(off, ck), :]
        acc = jnp.zeros((qb, ck), F32)
        for h in range(IDX_HEADS):
            s = _dot_nt(iq_ref[0, :, h * IDX_DIM:(h + 1) * IDX_DIM], ikc)
            acc = acc + jnp.maximum(s, 0.0) * iw[:, h:h + 1]
        key_scr[c] = jnp.where(kpos <= qpos, _sort_key(acc), INT_MIN)
        return carry

    lax.fori_loop(0, nkc, score_chunk, 0)

    def fold(m):
        s = m[:, :LANES]
        for j in range(1, ck // LANES):
            s = s + m[:, j * LANES:(j + 1) * LANES]
        return s

    def count_ge(cand):
        def body(c, acc):
            return acc + fold((key_scr[c] >= cand).astype(F32))
        return jnp.sum(lax.fori_loop(0, nkc, body, jnp.zeros((qb, LANES), F32)), axis=1, keepdims=True)

    def count_tie_below(t, j):
        def body(c, acc):
            _, kpos = chunk_pos(c)
            hit = (key_scr[c] == t) & (kpos < j)
            return acc + fold(hit.astype(F32))
        return jnp.sum(lax.fori_loop(0, nkc, body, jnp.zeros((qb, LANES), F32)), axis=1, keepdims=True)

    t, j = _select_threshold(count_ge, count_tie_below, qb, topk, pos_bits)

    def bias_chunk(c, carry):
        _, kpos = chunk_pos(c)
        key = key_scr[c]
        sel = ((key > t) | ((key == t) & (kpos <= j))) & (kpos <= qpos)
        bias_scr[c] = jnp.where(sel, 0.0, NEG_BIG)
        return carry

    lax.fori_loop(0, nkc, bias_chunk, 0)

    m_scr[...] = jnp.full(m_scr.shape, NEG_BIG, F32)
    l_scr[...] = jnp.zeros(l_scr.shape, F32)
    acc_scr[...] = jnp.zeros(acc_scr.shape, F32)

    def att_chunk(c, carry):
        off, _ = chunk_pos(c)
        bias = bias_scr[c]
        for h in range(N_HEADS):
            hs = slice(h * HEAD_DIM, (h + 1) * HEAD_DIM)
            s = _dot_nt(q_ref[0, :, hs], kb_ref[0, pl.ds(off, ck), hs]) + bias
            m_prev = m_scr[h]
            m_new = jnp.maximum(m_prev, jnp.max(s, axis=1, keepdims=True))
            a = jnp.exp(m_prev - m_new)
            p = jnp.exp(s - m_new)
            l_scr[h] = a * l_scr[h] + jnp.sum(p, axis=1, keepdims=True)
            acc_scr[h] = a * acc_scr[h] + _dot(p.astype(BF16), vb_ref[0, pl.ds(off, ck), hs])
            m_scr[h] = m_new
        return carry

    lax.fori_loop(0, nkc, att_chunk, 0)
    o_ref[0] = jnp.concatenate([acc_scr[h] / l_scr[h] for h in range(N_HEADS)], axis=1).astype(o_ref.dtype)


def prompt_sparse_attention(aq, iq, iw, akb, avb, ikb):
    B, L, _ = aq.shape
    topk = min(TOPK_MAX, L // 4)
    ck = min(KEY_CHUNK, L)
    qb = Q_BLOCK
    kern = functools.partial(_dsa_prompt_kernel, topk=topk, ck=ck, pos_bits=max(1, (L - 1).bit_length()))
    return pl.pallas_call(
        kern,
        out_shape=jax.ShapeDtypeStruct((B, L, ATT_W), BF16),
        grid=(B, L // qb),
        in_specs=[pl.BlockSpec((1, qb, ATT_W), lambda b, i: (b, i, 0)),
                  pl.BlockSpec((1, qb, IDX_HEADS * IDX_DIM), lambda b, i: (b, i, 0)),
                  pl.BlockSpec((1, qb, LANES), lambda b, i: (b, i, 0)),
                  pl.BlockSpec((1, L, ATT_W), lambda b, i: (b, 0, 0)),
                  pl.BlockSpec((1, L, ATT_W), lambda b, i: (b, 0, 0)),
                  pl.BlockSpec((1, L, IDX_DIM), lambda b, i: (b, 0, 0))],
        out_specs=pl.BlockSpec((1, qb, ATT_W), lambda b, i: (b, i, 0)),
        scratch_shapes=[pltpu.VMEM((L // ck, qb, ck), I32), pltpu.VMEM((L // ck, qb, ck), F32),
                        pltpu.VMEM((N_HEADS, qb, 1), F32), pltpu.VMEM((N_HEADS, qb, 1), F32),
                        pltpu.VMEM((N_HEADS, qb, HEAD_DIM), F32)],
        compiler_params=_params("arbitrary", "arbitrary"), name="dsa_prompt",
    )(aq, iq, iw, akb, avb, ikb)


def _dsa_sample_kernel(pt_ref, q_ref, iq_ref, iw_ref, kn_ref, vn_ref, ikn_ref, *rest, n_pages, topk, pos_bits):
    ik_pages = rest[:n_pages]
    k_pages = rest[n_pages:2 * n_pages]
    v_pages = rest[2 * n_pages:3 * n_pages]
    o_ref = rest[3 * n_pages]
    past = n_pages * PAGE_SIZE
    nk = past + LANES
    rows = iq_ref.shape[1]

    iq = iq_ref[0]
    iwc = iw_ref[0]
    parts = []
    for p in range(n_pages):
        s = _dot_nt(iq, ik_pages[p][0, 0].astype(BF16))
        parts.append(jnp.sum(jnp.maximum(s, 0.0) * iwc, axis=0, keepdims=True))
    ikn = ikn_ref[0].astype(BF16).astype(F32)
    s_new = jnp.sum(iq.astype(F32) * ikn, axis=1, keepdims=True)
    s_new = jnp.sum(jnp.maximum(s_new, 0.0) * iwc, axis=0, keepdims=True)
    parts.append(jnp.broadcast_to(s_new, (1, LANES)))
    score = jnp.concatenate(parts, axis=1)
    kpos = lax.broadcasted_iota(I32, (1, nk), 1)
    admissible = kpos <= past
    key = jnp.where(admissible, _sort_key(score), INT_MIN)

    def count_ge(cand):
        return jnp.sum((key >= cand).astype(F32), axis=1, keepdims=True)

    def count_tie_below(t, j):
        return jnp.sum(((key == t) & (kpos < j)).astype(F32), axis=1, keepdims=True)

    t, j = _select_threshold(count_ge, count_tie_below, 1, topk, pos_bits)
    sel = ((key > t) | ((key == t) & (kpos <= j))) & admissible
    bias = jnp.where(sel, 0.0, NEG_BIG)

    hrow = lax.broadcasted_iota(I32, (N_HEADS, ATT_W), 0)
    hcol = lax.broadcasted_iota(I32, (N_HEADS, ATT_W), 1) // HEAD_DIM
    diag = hrow == hcol
    qbd = jnp.where(diag, jnp.broadcast_to(q_ref[0], (N_HEADS, ATT_W)), 0.0)
    qbd_b = qbd.astype(BF16)
    sp = [_dot_nt(qbd_b, k_pages[p][0, 0].astype(BF16)) for p in range(n_pages)]
    kn = kn_ref[0].astype(BF16).astype(F32)
    s_n = jnp.sum(qbd_b.astype(F32) * kn, axis=1, keepdims=True)
    sp.append(jnp.broadcast_to(s_n, (N_HEADS, LANES)))
    s = jnp.concatenate(sp, axis=1) + bias
    m = jnp.max(s, axis=1, keepdims=True)
    p_all = jnp.exp(s - m)
    l = jnp.sum(p_all, axis=1, keepdims=True)
    acc = jnp.zeros((N_HEADS, ATT_W), F32)
    for p in range(n_pages):
        acc = acc + _dot(p_all[:, p * PAGE_SIZE:(p + 1) * PAGE_SIZE].astype(BF16), v_pages[p][0, 0].astype(BF16))
    vn = vn_ref[0].astype(BF16).astype(F32)
    p_new = p_all[:, past:past + 1].astype(BF16).astype(F32)
    acc = acc + p_new * vn
    o = jnp.where(diag, acc / l, 0.0)
    o_ref[0] = jnp.sum(o, axis=0, keepdims=True).astype(o_ref.dtype)


def sample_sparse_attention(layer, aq, iq, iw, ak, av, ik, cache_k, cache_v, cache_idx_k, page_table):
    DB = aq.shape[0]
    n_pages = page_table.shape[1]
    past = n_pages * PAGE_SIZE
    topk = min(TOPK_MAX, (past + 1) // 4)
    rows = 8
    iq8 = jnp.pad(iq.reshape(DB, IDX_HEADS, IDX_DIM), ((0, 0), (0, rows - IDX_HEADS), (0, 0)))
    iw8 = jnp.pad(iw[:, :IDX_HEADS], ((0, 0), (0, rows - IDX_HEADS))).reshape(DB, rows, 1)
    n_phys = cache_k.shape[1]
    ck = cache_k.reshape(DEPTH, n_phys, PAGE_SIZE, ATT_W)
    cv = cache_v.reshape(DEPTH, n_phys, PAGE_SIZE, ATT_W)

    def row_spec(width):
        return pl.BlockSpec((1, 1, width), lambda b, pt: (b, 0, 0))

    def page_spec(width, p):
        return pl.BlockSpec((1, 1, PAGE_SIZE, width), lambda b, pt, p=p: (layer, pt[b, p], 0, 0))

    in_specs = [row_spec(ATT_W),
                pl.BlockSpec((1, rows, IDX_DIM), lambda b, pt: (b, 0, 0)),
                pl.BlockSpec((1, rows, 1), lambda b, pt: (b, 0, 0)),
                row_spec(ATT_W), row_spec(ATT_W), row_spec(IDX_DIM)]
    in_specs += [page_spec(IDX_DIM, p) for p in range(n_pages)]
    in_specs += [page_spec(ATT_W, p) for p in range(n_pages)]
    in_specs += [page_spec(ATT_W, p) for p in range(n_pages)]
    kern = functools.partial(_dsa_sample_kernel, n_pages=n_pages, topk=topk,
                             pos_bits=max(1, (past + LANES - 1).bit_length()))
    out = pl.pallas_call(
        kern,
        out_shape=jax.ShapeDtypeStruct((DB, 1, ATT_W), BF16),
        grid_spec=pltpu.PrefetchScalarGridSpec(
            num_scalar_prefetch=1, grid=(DB,), in_specs=in_specs,
            out_specs=pl.BlockSpec((1, 1, ATT_W), lambda b, pt: (b, 0, 0))),
        compiler_params=_params("arbitrary"), name="dsa_sample",
    )(page_table, aq.reshape(DB, 1, ATT_W), iq8, iw8, ak.reshape(DB, 1, ATT_W), av.reshape(DB, 1, ATT_W),
      ik.reshape(DB, 1, IDX_DIM), *([cache_idx_k] * n_pages), *([ck] * n_pages), *([cv] * n_pages))
    return out.reshape(DB, ATT_W)


def _retention_tables(C):
    lg = jnp.log(1.0 - 2.0 ** (-5.0 - jnp.arange(RET_HEADS, dtype=F32)))
    i = jnp.arange(C, dtype=F32)
    diff = i[:, None] - i[None, :]
    causal = diff >= 0
    decay = jnp.where(causal[None], jnp.exp(jnp.where(causal, diff, 0.0)[None] * lg[:, None, None]), 0.0)
    row = jnp.exp((i[None, :] + 1.0) * lg[:, None])
    wk = jnp.exp((C - 1.0 - i)[None, :] * lg[:, None])
    full = jnp.exp(C * lg)
    return (decay, jnp.broadcast_to(row[:, :, None], (RET_HEADS, C, RET_DV)),
            jnp.broadcast_to(wk[:, :, None], (RET_HEADS, C, RET_DK)),
            jnp.broadcast_to(full[:, None, None], (RET_HEADS, 1, RET_DV)))


def _ret_prompt_kernel(q_ref, k_ref, v_ref, dm_ref, rd_ref, kd_ref, gc_ref, o_ref, s_ref, st_scr):
    c = pl.program_id(1)

    @pl.when(c == 0)
    def _():
        st_scr[...] = jnp.zeros(st_scr.shape, F32)

    for h in range(RET_HEADS):
        ks = slice(h * RET_DK, (h + 1) * RET_DK)
        vs = slice(h * RET_DV, (h + 1) * RET_DV)
        qh = q_ref[0, :, ks]
        kh = k_ref[0, :, ks]
        vh = v_ref[0, :, vs]
        st = st_scr[h]
        sc = _dot_nt(qh, kh.astype(BF16)) * dm_ref[h]
        o = _dot(sc.astype(BF16), vh) + _dot(qh, st.astype(BF16)) * rd_ref[h]
        kw = (kh * kd_ref[h]).astype(BF16)
        st_scr[h] = gc_ref[h] * st + _dot_tn(kw, vh)
        o_ref[0, :, vs] = _norm_rows(o)

    @pl.when(c == pl.num_programs(1) - 1)
    def _():
        s_ref[0] = st_scr[...]


def prompt_retention(rq, rk, rv):
    B, L, _ = rq.shape
    C = RET_CHUNK
    dm, rd, kd, gc = _retention_tables(C)

    def full(a):
        return pl.BlockSpec(a.shape, lambda b, c: (0,) * a.ndim)

    return pl.pallas_call(
        _ret_prompt_kernel,
        out_shape=[jax.ShapeDtypeStruct((B, L, RET_V), F32),
                   jax.ShapeDtypeStruct((B, RET_HEADS, RET_DK, RET_DV), F32)],
        grid=(B, L // C),
        in_specs=[pl.BlockSpec((1, C, RET_QK), lambda b, c: (b, c, 0)),
                  pl.BlockSpec((1, C, RET_QK), lambda b, c: (b, c, 0)),
                  pl.BlockSpec((1, C, RET_V), lambda b, c: (b, c, 0)),
                  full(dm), full(rd), full(kd), full(gc)],
        out_specs=[pl.BlockSpec((1, C, RET_V), lambda b, c: (b, c, 0)),
                   pl.BlockSpec((1, RET_HEADS, RET_DK, RET_DV), lambda b, c: (b, 0, 0, 0))],
        scratch_shapes=[pltpu.VMEM((RET_HEADS, RET_DK, RET_DV), F32)],
        compiler_params=_params("arbitrary", "arbitrary"), name="retention_prompt",
    )(rq, rk, rv, dm, rd, kd, gc)


def _ret_sample_kernel(q_ref, k_ref, v_ref, st_ref, rd_ref, gc_ref, o_ref, s_ref):
    G = q_ref.shape[0]
    for g in range(G):
        outs = []
        for h in range(RET_HEADS):
            qc = q_ref[g, h]
            kc = k_ref[g, h]
            vr = v_ref[g, h]
            st = st_ref[0, g, h]
            qk = jnp.sum(qc * kc, axis=0, keepdims=True)
            o = qk * vr + jnp.sum(qc * st, axis=0, keepdims=True) * rd_ref[h]
            s_ref[g, h] = gc_ref[h] * st + kc * vr
            outs.append(_norm_rows(o))
        o_ref[pl.ds(g, 1), :] = jnp.concatenate(outs, axis=1)


def sample_retention(layer, rq, rk, rv, state_ret):
    DB = rq.shape[0]
    G = 8
    _, rd, _, gc = _retention_tables(1)
    qT = rq.reshape(DB, RET_HEADS, RET_DK, 1)
    kT = rk.reshape(DB, RET_HEADS, RET_DK, 1)
    v4 = rv.reshape(DB, RET_HEADS, 1, RET_DV)
    col = pl.BlockSpec((G, RET_HEADS, RET_DK, 1), lambda i: (i, 0, 0, 0))
    return pl.pallas_call(
        _ret_sample_kernel,
        out_shape=[jax.ShapeDtypeStruct((DB, RET_V), F32),
                   jax.ShapeDtypeStruct((DB, RET_HEADS, RET_DK, RET_DV), F32)],
        grid=(DB // G,),
        in_specs=[col, col,
                  pl.BlockSpec((G, RET_HEADS, 1, RET_DV), lambda i: (i, 0, 0, 0)),
                  pl.BlockSpec((1, G, RET_HEADS, RET_DK, RET_DV), lambda i: (layer, i, 0, 0, 0)),
                  pl.BlockSpec(rd.shape, lambda i: (0, 0, 0)),
                  pl.BlockSpec(gc.shape, lambda i: (0, 0, 0))],
        out_specs=[pl.BlockSpec((G, RET_V), lambda i: (i, 0)),
                   pl.BlockSpec((G, RET_HEADS, RET_DK, RET_DV), lambda i: (i, 0, 0, 0))],
        compiler_params=_params("arbitrary"), name="retention_sample",
    )(qT, kT, v4, state_ret, rd, gc)


_HALO = 32


def _conv_post(y, g_ref, b_ref):
    return _silu(_norm_rows(y) * g_ref[...] + b_ref[...])


def _conv_prompt_kernel(z_ref, dw_ref, db_ref, g_ref, b_ref, o_ref, zp_scr, *, tl, rs):
    @pl.when(pl.program_id(1) == 0)
    def _():
        zp_scr[0:_HALO, :] = jnp.zeros((_HALO, CONV_CH), F32)

    zp_scr[_HALO:_HALO + tl, :] = z_ref[0]
    first = _HALO - (CONV_W - 1)
    for r in range(tl // rs):
        acc = jnp.broadcast_to(db_ref[...], (rs, CONV_CH))
        for k in range(CONV_W):
            lo = first + k + r * rs
            acc = acc + zp_scr[lo:lo + rs, :] * dw_ref[k:k + 1, :]
        o_ref[0, r * rs:(r + 1) * rs, :] = _conv_post(acc, g_ref, b_ref).astype(o_ref.dtype)
    zp_scr[0:_HALO, :] = zp_scr[tl:tl + _HALO, :]


def prompt_conv(z, dw, db, ln_g, ln_b):
    B, L, _ = z.shape
    tl = min(512, L)
    rs = min(128, tl)
    vec = pl.BlockSpec((1, CONV_CH), lambda b, i: (0, 0))
    return pl.pallas_call(
        functools.partial(_conv_prompt_kernel, tl=tl, rs=rs),
        out_shape=jax.ShapeDtypeStruct((B, L, CONV_CH), BF16),
        grid=(B, L // tl),
        in_specs=[pl.BlockSpec((1, tl, CONV_CH), lambda b, i: (b, i, 0)),
                  pl.BlockSpec((CONV_W, CONV_CH), lambda b, i: (0, 0)), vec, vec, vec],
        out_specs=pl.BlockSpec((1, tl, CONV_CH), lambda b, i: (b, i, 0)),
        scratch_shapes=[pltpu.VMEM((_HALO + tl, CONV_CH), F32)],
        compiler_params=_params("arbitrary", "arbitrary"), name="conv_prompt",
    )(z, dw, db.reshape(1, -1), ln_g.reshape(1, -1), ln_b.reshape(1, -1))


def _conv_sample_kernel(z_ref, buf_ref, dw_ref, db_ref, g_ref, b_ref, o_ref, nb_ref):
    G = z_ref.shape[0]
    w_past = dw_ref[0:CONV_W - 1, :]
    w_last = dw_ref[CONV_W - 1:CONV_W, :]
    rows = []
    for g in range(G):
        buf = buf_ref[0, g]
        zr = z_ref[pl.ds(g, 1), :]
        rows.append(jnp.sum(buf * w_past, axis=0, keepdims=True) + zr * w_last + db_ref[...])
        nb_ref[g, 0:CONV_W - 2, :] = buf_ref[0, g, 1:CONV_W - 1, :]
        nb_ref[g, CONV_W - 2:CONV_W - 1, :] = zr
    o_ref[...] = _conv_post(jnp.concatenate(rows, axis=0), g_ref, b_ref).astype(o_ref.dtype)


def sample_conv(layer, z, state_conv, dw, db, ln_g, ln_b):
    DB = z.shape[0]
    G = 8
    vec = pl.BlockSpec((1, CONV_CH), lambda i: (0, 0))
    return pl.pallas_call(
        _conv_sample_kernel,
        out_shape=[jax.ShapeDtypeStruct((DB, CONV_CH), BF16),
                   jax.ShapeDtypeStruct((DB, CONV_W - 1, CONV_CH), F32)],
        grid=(DB // G,),
        in_specs=[pl.BlockSpec((G, CONV_CH), lambda i: (i, 0)),
                  pl.BlockSpec((1, G, CONV_W - 1, CONV_CH), lambda i: (layer, i, 0, 0)),
                  pl.BlockSpec((CONV_W, CONV_CH), lambda i: (0, 0)), vec, vec, vec],
        out_specs=[pl.BlockSpec((G, CONV_CH), lambda i: (i, 0)),
                   pl.BlockSpec((G, CONV_W - 1, CONV_CH), lambda i: (i, 0, 0))],
        compiler_params=_params("arbitrary"), name="conv_sample",
    )(z, state_conv, dw, db.reshape(1, -1), ln_g.reshape(1, -1), ln_b.reshape(1, -1))


def _merge_kernel(x_ref, oret_ref, rg_ref, act_ref, att_ref, gs_ref, g1_ref, sc2_ref, sh2_ref,
                  wr_ref, wc_ref, bc_ref, wa_ref, wo_ref, lg_ref, lb_ref, *rest):
    x1_ref, h_ref = rest[-2], rest[-1]
    D = D_MODEL
    y_ret = _dot((_silu(rg_ref[0]) * oret_ref[0]).astype(BF16), wr_ref[...])
    y_conv = _dot(act_ref[0], wc_ref[...]) + bc_ref[...]
    y_att = _dot(att_ref[0], wa_ref[...])
    m = gs_ref[0, :, 0:D] * y_ret + gs_ref[0, :, D:2 * D] * y_conv + gs_ref[0, :, 2 * D:3 * D] * y_att
    mix = _dot(m.astype(BF16), wo_ref[...])
    x1 = _norm_rows(ALPHA * x_ref[0] + g1_ref[0] * mix) * lg_ref[...] + lb_ref[...]
    x1_ref[0] = x1
    h_ref[...] = x1 * (1.0 + sc2_ref[0]) + sh2_ref[0]


def merge_and_norm(x, oret, rg, act, att, gsig, g1, sc2, sh2, w_ret, w_conv, b_conv, w_att, w_o, ln_g, ln_b,
                   h_all, n_all, row0, tm):
    B, L, D = x.shape
    nl = L // tm
    blk0 = row0 // tm

    def rows(width):
        return pl.BlockSpec((1, tm, width), lambda b, i: (b, i, 0))

    def full(a):
        return pl.BlockSpec(a.shape, lambda b, i: (0,) * a.ndim)

    vec = pl.BlockSpec((1, D), lambda b, i: (0, 0))
    h_spec = pl.BlockSpec((tm, D), lambda b, i: (blk0 + b * nl + i, 0))
    ms = _mod_spec(g1, tm)
    args = [x, oret, rg, act, att, gsig, g1, sc2, sh2, w_ret, w_conv, b_conv.reshape(1, D), w_att, w_o,
            ln_g.reshape(1, D), ln_b.reshape(1, D)]
    in_specs = [rows(D), rows(RET_V), rows(RET_V), rows(CONV_CH), rows(ATT_W), rows(N_BRANCH * D), ms, ms, ms,
                full(w_ret), full(w_conv), vec, full(w_att), full(w_o), vec, vec]
    aliases = {}
    if h_all is not None:
        args.append(h_all)
        in_specs.append(pl.BlockSpec(memory_space=pl.ANY))
        aliases = {len(args) - 1: 1}
    return pl.pallas_call(
        _merge_kernel,
        out_shape=[jax.ShapeDtypeStruct((B, L, D), F32), jax.ShapeDtypeStruct((n_all, D), F32)],
        grid=(B, nl), in_specs=in_specs,
        out_specs=[rows(D), h_spec],
        input_output_aliases=aliases,
        compiler_params=_params("arbitrary", "arbitrary"), name="merge_norm",
    )(*args)


def _router_kernel(h_ref, whi_ref, wlo_ref, b_ref, e_ref, g_ref):
    h = h_ref[...]
    h_hi = h.astype(BF16)
    h_lo = (h - h_hi.astype(F32)).astype(BF16)
    logits = (_dot(h_hi, whi_ref[...]) + (_dot(h_hi, wlo_ref[...]) + _dot(h_lo, whi_ref[...]))) + b_ref[...]
    lane = lax.broadcasted_iota(I32, logits.shape, 1).astype(F32)
    vals = logits
    e_out = jnp.zeros(logits.shape, F32)
    top = []
    for k in range(MOE_TOP_K):
        m = jnp.max(vals, axis=1, keepdims=True)
        idx = jnp.min(jnp.where(vals == m, lane, float(LANES)), axis=1, keepdims=True)
        e_out = jnp.where(lane == float(k), idx, e_out)
        vals = jnp.where(lane == idx, -jnp.inf, vals)
        top.append(m)
    ex = [jnp.exp(t - top[0]) for t in top]
    den = ex[0] + ex[1] + ex[2] + ex[3]
    g_out = jnp.zeros(logits.shape, F32)
    for k in range(MOE_TOP_K):
        g_out = jnp.where(lane == float(k), ex[k] / den, g_out)
    e_ref[...] = e_out.astype(I32)
    g_ref[...] = g_out


def router_top4(h_all, router_w, router_b, tm):
    N, D = h_all.shape
    wp = jnp.zeros((D, LANES), F32).at[:, :N_EXPERTS].set(router_w)
    w_hi = wp.astype(BF16)
    w_lo = (wp - w_hi.astype(F32)).astype(BF16)
    bp = jnp.full((1, LANES), NEG_BIG, F32).at[0, :N_EXPERTS].set(router_b)
    e, g = pl.pallas_call(
        _router_kernel,
        out_shape=[jax.ShapeDtypeStruct((N, LANES), I32), jax.ShapeDtypeStruct((N, LANES), F32)],
        grid=(N // tm,),
        in_specs=[pl.BlockSpec((tm, D), lambda i: (i, 0)),
                  pl.BlockSpec((D, LANES), lambda i: (0, 0)),
                  pl.BlockSpec((D, LANES), lambda i: (0, 0)),
                  pl.BlockSpec((1, LANES), lambda i: (0, 0))],
        out_specs=[pl.BlockSpec((tm, LANES), lambda i: (i, 0)), pl.BlockSpec((tm, LANES), lambda i: (i, 0))],
        compiler_params=_params("arbitrary"), name="router_top4",
    )(h_all, w_hi, w_lo, bp)
    return e[:, :MOE_TOP_K], g[:, :MOE_TOP_K]


def _moe_kernel(blk_e_ref, nreal_ref, idx_hbm, h_hbm, g_ref, wgu_ref, bgu_ref, wd_ref, bd_ref, y_hbm,
                idx_smem, xbuf, obuf, hid_scr, wgu_bf, wd_bf, sem_idx, sem_g, sem_s, *, bm, nc):
    i = pl.program_id(0)
    n = pl.num_programs(0)

    def idx_copy(c):
        slot = c % 3
        return pltpu.make_async_copy(idx_hbm.at[c], idx_smem.at[slot], sem_idx.at[slot])

    def issue_gather(c):
        slot = c % 3
        buf = c % 2

        @pl.when(nreal_ref[c] > 0)
        def _():
            def body(r, carry):
                tok = idx_smem[slot, r]
                pltpu.make_async_copy(h_hbm.at[pl.ds(tok, 1)], xbuf.at[buf, pl.ds(r, 1)], sem_g.at[buf]).start()
                return carry

            lax.fori_loop(0, bm, body, 0)

    def wait_gather(c):
        buf = c % 2

        @pl.when(nreal_ref[c] > 0)
        def _():
            pltpu.make_async_copy(h_hbm.at[pl.ds(0, bm)], xbuf.at[buf], sem_g.at[buf]).wait()

    def wait_scatter(c):
        @pl.when(nreal_ref[c] > 0)
        def _():
            pltpu.make_async_copy(obuf, y_hbm.at[pl.ds(0, bm)], sem_s).wait()

    @pl.when(i == 0)
    def _():
        idx_copy(0).start()
        idx_copy(0).wait()
        issue_gather(0)

        @pl.when(n > 1)
        def _():
            idx_copy(1).start()

    @pl.when(i + 1 < n)
    def _():
        idx_copy(i + 1).wait()
        issue_gather(i + 1)

        @pl.when(i + 2 < n)
        def _():
            idx_copy(i + 2).start()

    nr = nreal_ref[i]
    buf = i % 2
    wait_gather(i)

    @pl.when((i == 0) | (blk_e_ref[i] != blk_e_ref[jnp.maximum(i - 1, 0)]))
    def _():
        wgu_bf[...] = wgu_ref[0].astype(BF16)
        wd_bf[...] = wd_ref[0].astype(BF16)

    @pl.when(i > 0)
    def _():
        wait_scatter(jnp.maximum(i - 1, 0))

    @pl.when(nr > 0)
    def _():
        x = xbuf[buf].astype(BF16)
        for j in range(D_EXPERT // nc):
            lo = j * nc
            a = _dot(x, wgu_bf[:, lo:lo + nc]) + bgu_ref[0, :, lo:lo + nc]
            lin = _dot(x, wgu_bf[:, D_EXPERT + lo:D_EXPERT + lo + nc]) + bgu_ref[0, :, D_EXPERT + lo:D_EXPERT + lo + nc]
            a = jnp.minimum(a, SWIGLU_LIMIT)
            lin = jnp.clip(lin, -SWIGLU_LIMIT, SWIGLU_LIMIT)
            hid_scr[:, lo:lo + nc] = (a * jax.nn.sigmoid(SWIGLU_ALPHA * a) * (lin + 1.0)).astype(BF16)
        obuf[...] = (_dot(hid_scr[...], wd_bf[...]) + bd_ref[0]) * g_ref[0]

        slot = i % 3

        def body(r, carry):
            dst = idx_smem[slot, bm + r]
            pltpu.make_async_copy(obuf.at[pl.ds(r, 1)], y_hbm.at[pl.ds(dst, 1)], sem_s).start()
            return carry

        lax.fori_loop(0, bm, body, 0)

    @pl.when(i == n - 1)
    def _():
        wait_scatter(i)


def moe_dispatch(top_e, gate, bm):
    n_tok = top_e.shape[0]
    n_assign = n_tok * MOE_TOP_K
    flat_e = top_e.reshape(-1)
    order = jnp.argsort(flat_e).astype(I32)
    se = flat_e[order]
    stok = order // MOE_TOP_K
    sk = order % MOE_TOP_K
    sg = gate.reshape(-1)[order]
    counts = jnp.bincount(flat_e, length=N_EXPERTS).astype(I32)
    starts = jnp.cumsum(counts) - counts
    rank = jnp.arange(n_assign, dtype=I32) - starts[se]
    pcounts = (counts + bm - 1) // bm * bm
    pends = jnp.cumsum(pcounts)
    pstarts = pends - pcounts
    ppos = pstarts[se] + rank
    n_chunks = -(-n_assign // bm) + N_EXPERTS
    n_slots = n_chunks * bm
    buf_tok = jnp.zeros((n_slots,), I32).at[ppos].set(stok)
    spare = n_assign + jnp.arange(n_slots, dtype=I32) % bm
    buf_dst = spare.at[ppos].set(sk * n_tok + stok)
    buf_g = jnp.zeros((n_slots,), F32).at[ppos].set(sg)
    chunk0 = jnp.arange(n_chunks, dtype=I32) * bm
    blk_e = jnp.minimum(jnp.searchsorted(pends, chunk0, side='right'), N_EXPERTS - 1).astype(I32)
    nreal = jnp.clip(pstarts[blk_e] + counts[blk_e] - chunk0, 0, bm).astype(I32)
    idx = jnp.concatenate([buf_tok.reshape(n_chunks, bm), buf_dst.reshape(n_chunks, bm)], axis=1)
    return blk_e, nreal, idx, buf_g.reshape(n_chunks, bm, 1)


def moe_experts(h_all, top_e, gate, layer, exp_w_gu, exp_b_gu, exp_w_down, exp_b_down):
    N, D = h_all.shape
    bm = MOE_BM
    nc = 256
    blk_e, nreal, idx, gcol = moe_dispatch(top_e, gate, bm)
    n_chunks = idx.shape[0]
    b_gu = exp_b_gu.reshape(DEPTH, N_EXPERTS, 1, 2 * D_EXPERT)
    b_dn = exp_b_down.reshape(DEPTH, N_EXPERTS, 1, D)
    grid_spec = pltpu.PrefetchScalarGridSpec(
        num_scalar_prefetch=2, grid=(n_chunks,),
        in_specs=[pl.BlockSpec(memory_space=pl.ANY),
                  pl.BlockSpec(memory_space=pl.ANY),
                  pl.BlockSpec((1, bm, 1), lambda i, be, nr: (i, 0, 0)),
                  pl.BlockSpec((1, 1, D, 2 * D_EXPERT), lambda i, be, nr: (layer, be[i], 0, 0)),
                  pl.BlockSpec((1, 1, 1, 2 * D_EXPERT), lambda i, be, nr: (layer, be[i], 0, 0)),
                  pl.BlockSpec((1, 1, D_EXPERT, D), lambda i, be, nr: (layer, be[i], 0, 0)),
                  pl.BlockSpec((1, 1, 1, D), lambda i, be, nr: (layer, be[i], 0, 0))],
        out_specs=pl.BlockSpec(memory_space=pl.ANY),
        scratch_shapes=[pltpu.SMEM((3, 2 * bm), I32),
                        pltpu.VMEM((2, bm, D), F32),
                        pltpu.VMEM((bm, D), F32),
                        pltpu.VMEM((bm, D_EXPERT), BF16),
                        pltpu.VMEM((D, 2 * D_EXPERT), BF16),
                        pltpu.VMEM((D_EXPERT, D), BF16),
                        pltpu.SemaphoreType.DMA((3,)),
                        pltpu.SemaphoreType.DMA((2,)),
                        pltpu.SemaphoreType.DMA(())])
    y = pl.pallas_call(
        functools.partial(_moe_squeeze_kernel, bm=bm, nc=nc),
        out_shape=jax.ShapeDtypeStruct((MOE_TOP_K * N + bm, D), F32),
        grid_spec=grid_spec,
        compiler_params=_params("arbitrary"), name="moe_experts",
    )(blk_e, nreal, idx, h_all, gcol, exp_w_gu, b_gu, exp_w_down, b_dn)
    return y


def _moe_squeeze_kernel(blk_e_ref, nreal_ref, idx_hbm, h_hbm, g_ref, wgu_ref, bgu_ref, wd_ref, bd_ref, y_hbm,
                        *scratch, bm, nc):
    _moe_kernel(blk_e_ref, nreal_ref, idx_hbm, h_hbm, g_ref, wgu_ref.at[0], bgu_ref.at[0], wd_ref.at[0],
                bd_ref.at[0], y_hbm, *scratch, bm=bm, nc=nc)


def _combine_kernel(x_ref, y0_ref, y1_ref, y2_ref, y3_ref, g2_ref, lg_ref, lb_ref, o_ref):
    ff = (y0_ref[...] + y1_ref[...]) + (y2_ref[...] + y3_ref[...])
    o_ref[0] = _norm_rows(ALPHA * x_ref[0] + g2_ref[0] * ff) * lg_ref[...] + lb_ref[...]


def combine_and_norm(x1, y, n_all, g2, ln_g, ln_b, row0, tm):
    B, L, D = x1.shape
    nl = L // tm
    blk0 = row0 // tm
    per_k = n_all // tm
    vec = pl.BlockSpec((1, D), lambda b, i: (0, 0))

    def y_spec(k):
        return pl.BlockSpec((tm, D), lambda b, i, k=k: (k * per_k + blk0 + b * nl + i, 0))

    return pl.pallas_call(
        _combine_kernel,
        out_shape=jax.ShapeDtypeStruct((B, L, D), F32),
        grid=(B, nl),
        in_specs=[pl.BlockSpec((1, tm, D), lambda b, i: (b, i, 0))] + [y_spec(k) for k in range(MOE_TOP_K)]
        + [_mod_spec(g2, tm), vec, vec],
        out_specs=pl.BlockSpec((1, tm, D), lambda b, i: (b, i, 0)),
        compiler_params=_params("arbitrary", "arbitrary"), name="combine_norm",
    )(x1, y, y, y, y, g2, ln_g.reshape(1, D), ln_b.reshape(1, D))


def _rope_tables(pos):
    half = HEAD_DIM // 2
    inv = 1.0 / (ROPE_THETA ** (jnp.arange(half, dtype=F32) / half))
    ang = pos.astype(F32)[:, None] * inv[None, :]
    cos, sin = jnp.cos(ang), jnp.sin(ang)
    reps = LANES // HEAD_DIM
    return jnp.tile(jnp.concatenate([cos, cos], axis=1), (1, reps)), jnp.tile(jnp.concatenate([-sin, sin], axis=1), (1, reps))


def _split_w_in(w):
    o = IN_OFFSETS
    aq, ak, av, iq, ik, iw, rq, rk, rv, rg, cglu, gl = [w[:, o[i]:o[i + 1]] for i in range(len(IN_SPLITS))]
    D = w.shape[0]
    pad_ik = jnp.zeros((D, _R_END - _R_IK - IDX_DIM), F32)
    pad_iw = jnp.zeros((D, _P_END - _P_IW - IDX_HEADS), F32)
    w_rope = jnp.concatenate([aq * (HEAD_DIM ** -0.5), ak, iq, rq, rk * (RET_DK ** -0.5), ik, pad_ik], axis=1)
    w_plain = jnp.concatenate([av, rv, rg, cglu, iw * ((IDX_DIM ** -0.5) * (IDX_HEADS ** -0.5)), pad_iw], axis=1)
    return w_rope.astype(BF16), w_plain.astype(BF16), gl.astype(BF16)


def kernel(x_prompt, x_sample, c_prompt, c_sample, cache_k, cache_v, cache_idx_k, state_ret, state_conv, page_table, ada_w, ada_b, w_in, w_ret_out, conv_dw, conv_db, conv_ln_g, conv_ln_b, w_conv_out, b_conv_out, w_att_out, w_o, ln1_g, ln1_b, router_w, router_b, exp_w_gu, exp_b_gu, exp_w_down, exp_b_down, ln2_g, ln2_b):
    B, L, D = x_prompt.shape
    DB = x_sample.shape[0]
    n_p = B * L
    n_all = n_p + DB
    past = page_table.shape[1] * PAGE_SIZE

    r_pad = -(B + DB) % 8
    c_all = jnp.concatenate([c_prompt, c_sample, jnp.zeros((r_pad, D), F32)], axis=0)
    mod = ada_modulation(c_all, ada_w, ada_b)

    cos_p, sin_p = _rope_tables(jnp.arange(L, dtype=I32))
    cos_s, sin_s = _rope_tables(jnp.full((DB,), past, I32))

    xp = x_prompt
    xs = x_sample.reshape(1, DB, D)
    new_p = ([], [], [], [], [])
    new_s = ([], [], [], [], [])
    tm_p = min(512, L)
    tm_m = min(256, L)
    tm_r = math.gcd(n_all, 384)
    tm_c = math.gcd(math.gcd(L, DB), 128)

    for l in range(DEPTH):
        mp = [mod[l, :B, i * D:(i + 1) * D].reshape(B, 1, D) for i in range(6)]
        ms = [mod[l, B:B + DB, i * D:(i + 1) * D].reshape(1, DB, D) for i in range(6)]
        w_rope, w_plain, w_gate = _split_w_in(w_in[l])
        wr, wc, wa, wo = (w_ret_out[l].astype(BF16), w_conv_out[l].astype(BF16), w_att_out[l].astype(BF16),
                          w_o[l].astype(BF16))

        pp = input_projection(xp, mp[1], mp[0], cos_p, sin_p, w_rope, w_plain, w_gate, tm_p)
        ps = input_projection(xs, ms[1], ms[0], cos_s, sin_s, w_rope, w_plain, w_gate, DB)

        att_p = prompt_sparse_attention(pp['aq'], pp['iq'], pp['iw'], pp['akb'], pp['avb'], pp['ikb'])
        oret_p, sret_p = prompt_retention(pp['rq'], pp['rk'], pp['rv'])
        act_p = prompt_conv(pp['z'], conv_dw[l], conv_db[l], conv_ln_g[l], conv_ln_b[l])

        att_s = sample_sparse_attention(l, ps['aq'][0].astype(F32), ps['iq'][0], ps['iw'][0], ps['ak'][0],
                                        ps['av'][0], ps['ik'][0], cache_k, cache_v, cache_idx_k, page_table)
        oret_s, sret_s = sample_retention(l, ps['rq'][0].astype(F32), ps['rk'][0], ps['rv'][0].astype(F32), state_ret)
        act_s, conv_s = sample_conv(l, ps['z'][0], state_conv, conv_dw[l], conv_db[l], conv_ln_g[l], conv_ln_b[l])

        x1p, h_all = merge_and_norm(xp, oret_p, pp['rg'], act_p, att_p, pp['gsig'], mp[2], mp[4], mp[3],
                                    wr, wc, b_conv_out[l], wa, wo, ln1_g[l], ln1_b[l], None, n_all, 0, tm_m)
        x1s, h_all = merge_and_norm(xs, oret_s.reshape(1, DB, -1), ps['rg'], act_s.reshape(1, DB, -1),
                                    att_s.reshape(1, DB, -1), ps['gsig'], ms[2], ms[4], ms[3],
                                    wr, wc, b_conv_out[l], wa, wo, ln1_g[l], ln1_b[l], h_all, n_all, n_p, DB)

        top_e, gate = router_top4(h_all, router_w[l], router_b[l], tm_r)
        y4 = moe_experts(h_all, top_e, gate, l, exp_w_gu, exp_b_gu, exp_w_down, exp_b_down)
        xp = combine_and_norm(x1p, y4, n_all, mp[5], ln2_g[l], ln2_b[l], 0, tm_c)
        xs = combine_and_norm(x1s, y4, n_all, ms[5], ln2_g[l], ln2_b[l], n_p, tm_c)

        for lst, a in zip(new_p, (pp['ak'].reshape(B, L, N_HEADS, HEAD_DIM), pp['av'].reshape(B, L, N_HEADS, HEAD_DIM),
                                  pp['ik'], sret_p, pp['z'][:, L - (CONV_W - 1):, :])):
            lst.append(a)
        for lst, a in zip(new_s, (ps['ak'].reshape(DB, 1, N_HEADS, HEAD_DIM), ps['av'].reshape(DB, 1, N_HEADS, HEAD_DIM),
                                  ps['ik'].reshape(DB, 1, IDX_DIM), sret_s, conv_s)):
            lst.append(a)

    nk_p, nv_p, nik_p, nr_p, nc_p = [jnp.stack(a) for a in new_p]
    nk_s, nv_s, nik_s, nr_s, nc_s = [jnp.stack(a) for a in new_s]
    return (xp, xs.reshape(DB, 1, D), nk_p, nv_p, nik_p, nr_p, nc_p, nk_s, nv_s, nik_s, nr_s, nc_s)
```

```python
import functools
import math

import numpy as np
import jax
import jax.numpy as jnp
from jax import lax
from jax.experimental import pallas as pl
from jax.experimental.pallas import tpu as pltpu

F32 = jnp.float32
BF16 = jnp.bfloat16
I32 = jnp.int32

D_MODEL = 1024
DEPTH = 4
PAGE_SIZE = 128
N_HEADS = 8
HEAD_DIM = 64
ATT_W = N_HEADS * HEAD_DIM
IDX_HEADS = 4
IDX_DIM = 64
TOPK_MAX = 256
Q_BLOCK = 128
ROPE_THETA = 10000.0
RET_HEADS = 4
RET_DK = 64
RET_DV = 128
RET_QK = RET_HEADS * RET_DK
RET_V = RET_HEADS * RET_DV
RET_CHUNK = 128
CONV_CH = 512
CONV_W = 31
N_EXPERTS = 32
MOE_TOP_K = 4
D_EXPERT = 1024
SWIGLU_LIMIT = 7.0
SWIGLU_ALPHA = 1.702
N_BRANCH = 3
ALPHA = (2 * DEPTH) ** 0.25
LN_EPS = 1e-5
IN_SPLITS = (ATT_W, ATT_W, ATT_W, IDX_HEADS * IDX_DIM, IDX_DIM, IDX_HEADS, RET_QK, RET_QK, RET_V, RET_V,
             2 * CONV_CH, N_BRANCH * D_MODEL)
IN_OFFSETS = tuple(int(o) for o in np.cumsum((0,) + IN_SPLITS))

LANES = 128
INT_MIN = -2 ** 31
NEG_BIG = -1e30
MOE_BM = 256
PROJ_TM = 512
HEAD_GROUP = 4
VMEM_LIMIT = 56 * 1024 * 1024


def _dot(a, b):
    return jnp.dot(a, b, preferred_element_type=F32)


def _dot_nt(a, b):
    return lax.dot_general(a, b, (((1,), (1,)), ((), ())), preferred_element_type=F32)


def _dot_tn(a, b):
    return lax.dot_general(a, b, (((0,), (0,)), ((), ())), preferred_element_type=F32)


def _silu(x):
    return x * jax.nn.sigmoid(x)


def _norm_rows(x):
    mu = jnp.mean(x, axis=-1, keepdims=True)
    xc = x - mu
    var = jnp.mean(xc * xc, axis=-1, keepdims=True)
    return xc * lax.rsqrt(var + LN_EPS)


def _params(*sem):
    return pltpu.CompilerParams(dimension_semantics=sem, vmem_limit_bytes=VMEM_LIMIT)


def _ada_kernel(c_ref, w_ref, b_ref, o_ref):
    c = c_ref[...]
    o_ref[0] = _dot(_silu(c).astype(BF16), w_ref[0].astype(BF16)) + b_ref[0]


def ada_modulation(c_all, ada_w, ada_b):
    R, D = c_all.shape
    W = ada_w.shape[-1]
    tn = 1536
    return pl.pallas_call(
        _ada_kernel,
        out_shape=jax.ShapeDtypeStruct((DEPTH, R, W), F32),
        grid=(DEPTH, W // tn),
        in_specs=[pl.BlockSpec((R, D), lambda l, j: (0, 0)),
                  pl.BlockSpec((1, D, tn), lambda l, j: (l, 0, j)),
                  pl.BlockSpec((1, 1, tn), lambda l, j: (l, 0, j))],
        out_specs=pl.BlockSpec((1, R, tn), lambda l, j: (l, 0, j)),
        compiler_params=_params("arbitrary", "arbitrary"),
        name="ada_modulation",
    )(c_all, ada_w, ada_b.reshape(DEPTH, 1, W))


def _rope_chunks(y, cos, sin, first_half):
    parts = []
    for c in range(y.shape[1] // LANES):
        yc = y[:, c * LANES:(c + 1) * LANES]
        rot = jnp.where(first_half, pltpu.roll(yc, LANES - HEAD_DIM // 2, 1), pltpu.roll(yc, HEAD_DIM // 2, 1))
        parts.append(yc * cos + rot * sin)
    return parts[0] if len(parts) == 1 else jnp.concatenate(parts, axis=1)


_R_AQ, _R_AK, _R_IQ, _R_RQ, _R_RK, _R_IK, _R_END = 0, 512, 1024, 1280, 1536, 1792, 1920


def _proj_rope_kernel(x_ref, sc_ref, sh_ref, cos_ref, sin_ref, w_ref,
                      aq_ref, ak_ref, akt_ref, iq_ref, rq_ref, rk_ref, ik_ref, ikt_ref):
    u = (x_ref[0] * (1.0 + sc_ref[0]) + sh_ref[0]).astype(BF16)
    cos = cos_ref[...]
    sin = sin_ref[...]
    lane = lax.broadcasted_iota(I32, cos.shape, 1)
    first_half = (lane % HEAD_DIM) < (HEAD_DIM // 2)

    def proj(lo, hi):
        return _rope_chunks(_dot(u, w_ref[:, lo:hi]), cos, sin, first_half)

    aq = proj(_R_AQ, _R_AK).astype(BF16)
    for h in range(N_HEADS):
        aq_ref[0, h] = aq[:, h * HEAD_DIM:(h + 1) * HEAD_DIM]
    ak = proj(_R_AK, _R_IQ)
    ak_ref[0] = ak
    akt_ref[0, 0] = ak.T.astype(BF16)
    iq = proj(_R_IQ, _R_RQ).astype(BF16)
    for h in range(IDX_HEADS):
        iq_ref[0, h] = iq[:, h * IDX_DIM:(h + 1) * IDX_DIM]
    rq_ref[0] = proj(_R_RQ, _R_RK).astype(BF16)
    rk_ref[0] = proj(_R_RK, _R_IK)
    ik2 = proj(_R_IK, _R_END)
    ik_ref[0] = ik2[:, :IDX_DIM]
    ikt_ref[0, 0] = ik2.T[:IDX_DIM, :].astype(BF16)


_P_AV, _P_RV, _P_RG, _P_CA, _P_CG, _P_IW, _P_END = 0, 512, 1024, 1536, 2048, 2560, 2688


def _proj_plain_kernel(x_ref, sc_ref, sh_ref, w_ref, av_ref, avh_ref, rv_ref, rg_ref, z_ref, iw_ref):
    u = (x_ref[0] * (1.0 + sc_ref[0]) + sh_ref[0]).astype(BF16)
    av = _dot(u, w_ref[:, _P_AV:_P_RV])
    av_ref[0] = av
    ones = jnp.ones((av.shape[0], LANES - HEAD_DIM), BF16)
    avb = av.astype(BF16)
    for h in range(N_HEADS):
        avh_ref[0, h] = jnp.concatenate([avb[:, h * HEAD_DIM:(h + 1) * HEAD_DIM], ones], axis=1)
    rv_ref[0] = _dot(u, w_ref[:, _P_RV:_P_RG]).astype(BF16)
    rg_ref[0] = _dot(u, w_ref[:, _P_RG:_P_CA])
    a = _dot(u, w_ref[:, _P_CA:_P_CG])
    g = _dot(u, w_ref[:, _P_CG:_P_IW])
    z_ref[0] = a * jax.nn.sigmoid(g)
    iw_ref[0] = _dot(u, w_ref[:, _P_IW:_P_END])


def _proj_gate_kernel(x_ref, sc_ref, sh_ref, w_ref, gs_ref):
    u = (x_ref[0] * (1.0 + sc_ref[0]) + sh_ref[0]).astype(BF16)
    for c in range(N_BRANCH):
        gs_ref[0, :, c * D_MODEL:(c + 1) * D_MODEL] = jax.nn.sigmoid(
            _dot(u, w_ref[:, c * D_MODEL:(c + 1) * D_MODEL]))


def _mod_spec(mod, tm):
    if mod.shape[1] == 1:
        return pl.BlockSpec((1, 1, D_MODEL), lambda b, i: (b, 0, 0))
    return pl.BlockSpec((1, tm, D_MODEL), lambda b, i: (b, i, 0))


def input_projection(x, sc, sh, cos_t, sin_t, w_rope, w_plain, w_gate, tm):
    B, L, D = x.shape
    grid = (B, L // tm)
    xs = pl.BlockSpec((1, tm, D), lambda b, i: (b, i, 0))
    ms = _mod_spec(sc, tm)
    ts = pl.BlockSpec((tm, LANES), lambda b, i: (i, 0))

    def ws(w):
        return pl.BlockSpec(w.shape, lambda b, i: (0, 0))

    def o(width, dt):
        return jax.ShapeDtypeStruct((B, L, width), dt), pl.BlockSpec((1, tm, width), lambda b, i: (b, i, 0))

    nt = L // tm

    def heads(n, width, dt):
        return jax.ShapeDtypeStruct((B, n, L, width), dt), pl.BlockSpec((1, n, tm, width), lambda b, i: (b, 0, i, 0))

    def chunked_t(rows, dt):
        return jax.ShapeDtypeStruct((B, nt, rows, tm), dt), pl.BlockSpec((1, 1, rows, tm), lambda b, i: (b, i, 0, 0))

    outs = [heads(N_HEADS, HEAD_DIM, BF16), o(ATT_W, F32), chunked_t(ATT_W, BF16), heads(IDX_HEADS, IDX_DIM, BF16),
            o(RET_QK, BF16), o(RET_QK, F32), o(IDX_DIM, F32), chunked_t(IDX_DIM, BF16)]
    aq, ak, akt, iq, rq, rk, ik, ikt = pl.pallas_call(
        _proj_rope_kernel,
        out_shape=[s for s, _ in outs], grid=grid,
        in_specs=[xs, ms, ms, ts, ts, ws(w_rope)],
        out_specs=[s for _, s in outs],
        compiler_params=_params("arbitrary", "arbitrary"), name="proj_rope",
    )(x, sc, sh, cos_t, sin_t, w_rope)

    outs = [o(ATT_W, F32), heads(N_HEADS, LANES, BF16), o(RET_V, BF16), o(RET_V, F32), o(CONV_CH, F32), o(LANES, F32)]
    av, avh, rv, rg, z, iw = pl.pallas_call(
        _proj_plain_kernel,
        out_shape=[s for s, _ in outs], grid=grid,
        in_specs=[xs, ms, ms, ws(w_plain)],
        out_specs=[s for _, s in outs],
        compiler_params=_params("arbitrary", "arbitrary"), name="proj_plain",
    )(x, sc, sh, w_plain)

    gs_shape, gs_spec = o(N_BRANCH * D_MODEL, F32)
    gsig = pl.pallas_call(
        _proj_gate_kernel,
        out_shape=gs_shape, grid=grid,
        in_specs=[xs, ms, ms, ws(w_gate)],
        out_specs=gs_spec,
        compiler_params=_params("arbitrary", "arbitrary"), name="proj_gate",
    )(x, sc, sh, w_gate)
    return dict(aq=aq, ak=ak, akt=akt, iq=iq, rq=rq, rk=rk, ik=ik, ikt=ikt,
                av=av, avh=avh, rv=rv, rg=rg, z=z, iw=iw, gsig=gsig)


def _sort_key(score):
    score = jnp.where(score == 0.0, 0.0, score)
    b = pltpu.bitcast(score, I32)
    return b ^ ((b >> 31) & 0x7FFFFFFF)


def _topk_threshold(count_ge, rows, topk):
    k = float(topk)
    t = jnp.where(count_ge(jnp.zeros((rows, 1), I32)) >= k, 0, INT_MIN).astype(I32)

    def bit_step(i, t):
        cand = t | jnp.left_shift(1, 30 - i).astype(I32)
        return jnp.where(count_ge(cand) >= k, cand, t)

    return lax.fori_loop(0, 31, bit_step, t)


def _tie_bound(count_tie_below, t, need, rows, pos_bits):
    def pos_step(i, j):
        cand = j | jnp.left_shift(1, pos_bits - 1 - i).astype(I32)
        return jnp.where(count_tie_below(t, cand) < need, cand, j)

    return lax.fori_loop(0, pos_bits, pos_step, jnp.zeros((rows, 1), I32))


def _dsa_prompt_kernel(q_ref, iq_ref, iw_ref, kt_ref, vh_ref, ikt_ref, o_ref,
                       key_scr, bias_scr, j_scr, m_scr, acc_scr, *, topk, ck, pos_bits):
    qb = q_ref.shape[2]
    t0 = pl.program_id(1) * qb
    nkc = (t0 + qb + ck - 1) // ck
    qpos = t0 + lax.broadcasted_iota(I32, (qb, 1), 0)
    iw = iw_ref[0]
    iw_cols = [iw[:, h:h + 1] for h in range(IDX_HEADS)]

    def chunk_kpos(c):
        return c * ck + lax.broadcasted_iota(I32, (qb, ck), 1)

    def score_chunk(c, carry):
        ikc = ikt_ref[0, c]
        acc = jnp.zeros((qb, ck), F32)
        for h in range(IDX_HEADS):
            acc = acc + jnp.maximum(_dot(iq_ref[0, h], ikc), 0.0) * iw_cols[h]
        key_scr[c] = jnp.where(chunk_kpos(c) <= qpos, _sort_key(acc), INT_MIN)
        return carry

    lax.fori_loop(0, nkc, score_chunk, 0)

    def fold(m):
        s = m[:, :LANES]
        for j in range(1, ck // LANES):
            s = s + m[:, j * LANES:(j + 1) * LANES]
        return s

    def count_ge(cand):
        def body(c, acc):
            return acc + fold((key_scr[c] >= cand).astype(F32))
        return jnp.sum(lax.fori_loop(0, nkc, body, jnp.zeros((qb, LANES), F32)), axis=1, keepdims=True)

    def count_tie_below(t, j):
        def body(c, acc):
            hit = (key_scr[c] == t) & (chunk_kpos(c) < j)
            return acc + fold(hit.astype(F32))
        return jnp.sum(lax.fori_loop(0, nkc, body, jnp.zeros((qb, LANES), F32)), axis=1, keepdims=True)

    t = _topk_threshold(count_ge, qb, topk)
    n_gt = count_ge(t + 1)
    n_tie = count_ge(t) - n_gt
    need = float(topk) - n_gt
    j_scr[...] = jnp.full((qb, 1), 2 ** pos_bits - 1, I32)
    short = jnp.max(jnp.where((n_tie > need) & (t > INT_MIN), 1.0, 0.0))

    @pl.when(short > 0.0)
    def _():
        j_scr[...] = _tie_bound(count_tie_below, t, need, qb, pos_bits)

    j = j_scr[...]

    def bias_chunk(c, carry):
        kpos = chunk_kpos(c)
        key = key_scr[c]
        sel = ((key > t) | ((key == t) & (kpos <= j))) & (kpos <= qpos)
        bias_scr[c] = jnp.where(sel, 0.0, NEG_BIG)
        return carry

    lax.fori_loop(0, nkc, bias_chunk, 0)

    m_scr[...] = jnp.full(m_scr.shape, NEG_BIG, F32)
    acc_scr[...] = jnp.zeros(acc_scr.shape, F32)

    def att_chunk(c, carry):
        off = pl.multiple_of(c * ck, ck)
        bias = bias_scr[c]
        for g in range(0, N_HEADS, HEAD_GROUP):
            hs = range(g, g + HEAD_GROUP)
            s = [_dot(q_ref[0, h], kt_ref[0, c, h * HEAD_DIM:(h + 1) * HEAD_DIM, :]) + bias for h in hs]
            m_prev = [m_scr[h] for h in hs]
            m_new = [jnp.maximum(mp, jnp.max(si, axis=1, keepdims=True)) for mp, si in zip(m_prev, s)]
            p = [jnp.exp(si - mn).astype(BF16) for si, mn in zip(s, m_new)]
            pv = [_dot(pi, vh_ref[0, h, pl.ds(off, ck), :]) for pi, h in zip(p, hs)]
            for i, h in enumerate(hs):
                acc_scr[h] = jnp.exp(m_prev[i] - m_new[i]) * acc_scr[h] + pv[i]
                m_scr[h] = m_new[i]
        return carry

    lax.fori_loop(0, nkc, att_chunk, 0)
    outs = []
    for h in range(N_HEADS):
        acc = acc_scr[h]
        outs.append(acc[:, :HEAD_DIM] / acc[:, HEAD_DIM:2 * HEAD_DIM])
    o_ref[0] = jnp.concatenate(outs, axis=1).astype(o_ref.dtype)


def prompt_sparse_attention(aq, iq, iw, akt, avh, ikt):
    B, _, L, _ = aq.shape
    nc, _, ck = akt.shape[1:]
    topk = min(TOPK_MAX, L // 4)
    qb = Q_BLOCK
    kern = functools.partial(_dsa_prompt_kernel, topk=topk, ck=ck, pos_bits=max(1, (L - 1).bit_length()))
    return pl.pallas_call(
        kern,
        out_shape=jax.ShapeDtypeStruct((B, L, ATT_W), BF16),
        grid=(B, L // qb),
        in_specs=[pl.BlockSpec((1, N_HEADS, qb, HEAD_DIM), lambda b, i: (b, 0, i, 0)),
                  pl.BlockSpec((1, IDX_HEADS, qb, IDX_DIM), lambda b, i: (b, 0, i, 0)),
                  pl.BlockSpec((1, qb, LANES), lambda b, i: (b, i, 0)),
                  pl.BlockSpec((1, nc, ATT_W, ck), lambda b, i: (b, 0, 0, 0)),
                  pl.BlockSpec((1, N_HEADS, L, LANES), lambda b, i: (b, 0, 0, 0)),
                  pl.BlockSpec((1, nc, IDX_DIM, ck), lambda b, i: (b, 0, 0, 0))],
        out_specs=pl.BlockSpec((1, qb, ATT_W), lambda b, i: (b, i, 0)),
        scratch_shapes=[pltpu.VMEM((nc, qb, ck), I32), pltpu.VMEM((nc, qb, ck), F32),
                        pltpu.VMEM((qb, 1), I32),
                        pltpu.VMEM((N_HEADS, qb, 1), F32),
                        pltpu.VMEM((N_HEADS, qb, LANES), F32)],
        compiler_params=_params("arbitrary", "arbitrary"), name="dsa_prompt",
    )(aq, iq, iw, akt, avh, ikt)


def _dsa_sample_kernel(pt_ref, q_ref, iq_ref, iw_ref, kn_ref, vn_ref, ikn_ref, *rest, n_pages, topk, pos_bits):
    ik_pages = rest[:n_pages]
    k_pages = rest[n_pages:2 * n_pages]
    v_pages = rest[2 * n_pages:3 * n_pages]
    o_ref = rest[3 * n_pages]
    past = n_pages * PAGE_SIZE
    nk = past + LANES
    rows = iq_ref.shape[1]

    iq = iq_ref[0]
    iwc = iw_ref[0]
    parts = []
    for p in range(n_pages):
        s = _dot_nt(iq, ik_pages[p][0, 0].astype(BF16))
        parts.append(jnp.sum(jnp.maximum(s, 0.0) * iwc, axis=0, keepdims=True))
    ikn = ikn_ref[0].astype(BF16).astype(F32)
    s_new = jnp.sum(iq.astype(F32) * ikn, axis=1, keepdims=True)
    s_new = jnp.sum(jnp.maximum(s_new, 0.0) * iwc, axis=0, keepdims=True)
    parts.append(jnp.broadcast_to(s_new, (1, LANES)))
    score = jnp.concatenate(parts, axis=1)
    kpos = lax.broadcasted_iota(I32, (1, nk), 1)
    admissible = kpos <= past
    key = jnp.where(admissible, _sort_key(score), INT_MIN)

    def count_ge(cand):
        return jnp.sum((key >= cand).astype(F32), axis=1, keepdims=True)

    def count_tie_below(t, j):
        return jnp.sum(((key == t) & (kpos < j)).astype(F32), axis=1, keepdims=True)

    t = _topk_threshold(count_ge, 1, topk)
    j = _tie_bound(count_tie_below, t, float(topk) - count_ge(t + 1), 1, pos_bits)
    sel = ((key > t) | ((key == t) & (kpos <= j))) & admissible
    bias = jnp.where(sel, 0.0, NEG_BIG)

    hrow = lax.broadcasted_iota(I32, (N_HEADS, ATT_W), 0)
    hcol = lax.broadcasted_iota(I32, (N_HEADS, ATT_W), 1) // HEAD_DIM
    diag = hrow == hcol
    qbd = jnp.where(diag, jnp.broadcast_to(q_ref[0], (N_HEADS, ATT_W)), 0.0)
    qbd_b = qbd.astype(BF16)
    def page_rows(ref):
        return jnp.concatenate([ref[0, 0, :, h, :] for h in range(N_HEADS)], axis=1).astype(BF16)

    sp = [_dot_nt(qbd_b, page_rows(k_pages[p])) for p in range(n_pages)]
    kn = kn_ref[0].astype(BF16).astype(F32)
    s_n = jnp.sum(qbd_b.astype(F32) * kn, axis=1, keepdims=True)
    sp.append(jnp.broadcast_to(s_n, (N_HEADS, LANES)))
    s = jnp.concatenate(sp, axis=1) + bias
    m = jnp.max(s, axis=1, keepdims=True)
    p_all = jnp.exp(s - m)
    l = jnp.sum(p_all, axis=1, keepdims=True)
    acc = jnp.zeros((N_HEADS, ATT_W), F32)
    for p in range(n_pages):
        acc = acc + _dot(p_all[:, p * PAGE_SIZE:(p + 1) * PAGE_SIZE].astype(BF16), page_rows(v_pages[p]))
    vn = vn_ref[0].astype(BF16).astype(F32)
    p_new = p_all[:, past:past + 1].astype(BF16).astype(F32)
    acc = acc + p_new * vn
    o = jnp.where(diag, acc / l, 0.0)
    o_ref[0] = jnp.sum(o, axis=0, keepdims=True).astype(o_ref.dtype)


def sample_sparse_attention(layer, aq, iq, iw, ak, av, ik, cache_k, cache_v, cache_idx_k, page_table):
    DB = aq.shape[0]
    n_pages = page_table.shape[1]
    past = n_pages * PAGE_SIZE
    topk = min(TOPK_MAX, (past + 1) // 4)
    rows = 8
    iq8 = jnp.pad(iq.reshape(DB, IDX_HEADS, IDX_DIM), ((0, 0), (0, rows - IDX_HEADS), (0, 0)))
    iw8 = jnp.pad(iw[:, :IDX_HEADS], ((0, 0), (0, rows - IDX_HEADS))).reshape(DB, rows, 1)

    def row_spec(width):
        return pl.BlockSpec((1, 1, width), lambda b, pt: (b, 0, 0))

    def page_spec(width, p):
        return pl.BlockSpec((1, 1, PAGE_SIZE, width), lambda b, pt, p=p: (layer, pt[b, p], 0, 0))

    def kv_page_spec(p):
        return pl.BlockSpec((1, 1, PAGE_SIZE, N_HEADS, HEAD_DIM), lambda b, pt, p=p: (layer, pt[b, p], 0, 0, 0))

    in_specs = [row_spec(ATT_W),
                pl.BlockSpec((1, rows, IDX_DIM), lambda b, pt: (b, 0, 0)),
                pl.BlockSpec((1, rows, 1), lambda b, pt: (b, 0, 0)),
                row_spec(ATT_W), row_spec(ATT_W), row_spec(IDX_DIM)]
    in_specs += [page_spec(IDX_DIM, p) for p in range(n_pages)]
    in_specs += [kv_page_spec(p) for p in range(n_pages)]
    in_specs += [kv_page_spec(p) for p in range(n_pages)]
    kern = functools.partial(_dsa_sample_kernel, n_pages=n_pages, topk=topk,
                             pos_bits=max(1, (past + LANES - 1).bit_length()))
    out = pl.pallas_call(
        kern,
        out_shape=jax.ShapeDtypeStruct((DB, 1, ATT_W), BF16),
        grid_spec=pltpu.PrefetchScalarGridSpec(
            num_scalar_prefetch=1, grid=(DB,), in_specs=in_specs,
            out_specs=pl.BlockSpec((1, 1, ATT_W), lambda b, pt: (b, 0, 0))),
        compiler_params=_params("arbitrary"), name="dsa_sample",
    )(page_table, aq.reshape(DB, 1, ATT_W), iq8, iw8, ak.reshape(DB, 1, ATT_W), av.reshape(DB, 1, ATT_W),
      ik.reshape(DB, 1, IDX_DIM), *([cache_idx_k] * n_pages), *([cache_k] * n_pages), *([cache_v] * n_pages))
    return out.reshape(DB, ATT_W)


def _retention_tables(C):
    lg = jnp.log(1.0 - 2.0 ** (-5.0 - jnp.arange(RET_HEADS, dtype=F32)))
    i = jnp.arange(C, dtype=F32)
    diff = i[:, None] - i[None, :]
    causal = diff >= 0
    decay = jnp.where(causal[None], jnp.exp(jnp.where(causal, diff, 0.0)[None] * lg[:, None, None]), 0.0)
    row = jnp.exp((i[None, :] + 1.0) * lg[:, None])
    wk = jnp.exp((C - 1.0 - i)[None, :] * lg[:, None])
    full = jnp.exp(C * lg)
    return (decay, jnp.broadcast_to(row[:, :, None], (RET_HEADS, C, RET_DV)),
            jnp.broadcast_to(wk[:, :, None], (RET_HEADS, C, RET_DK)),
            jnp.broadcast_to(full[:, None, None], (RET_HEADS, 1, RET_DV)))


def _ret_prompt_kernel(q_ref, k_ref, v_ref, dm_ref, rd_ref, kd_ref, gc_ref, o_ref, s_ref, st_scr):
    c = pl.program_id(1)

    @pl.when(c == 0)
    def _():
        st_scr[...] = jnp.zeros(st_scr.shape, F32)

    for h in range(RET_HEADS):
        ks = slice(h * RET_DK, (h + 1) * RET_DK)
        vs = slice(h * RET_DV, (h + 1) * RET_DV)
        qh = q_ref[0, :, ks]
        kh = k_ref[0, :, ks]
        vh = v_ref[0, :, vs]
        st = st_scr[h]
        sc = _dot_nt(qh, kh.astype(BF16)) * dm_ref[h]
        o = _dot(sc.astype(BF16), vh) + _dot(qh, st.astype(BF16)) * rd_ref[h]
        kw = (kh * kd_ref[h]).astype(BF16)
        st_scr[h] = gc_ref[h] * st + _dot_tn(kw, vh)
        o_ref[0, :, vs] = _norm_rows(o)

    @pl.when(c == pl.num_programs(1) - 1)
    def _():
        s_ref[0] = st_scr[...]


def prompt_retention(rq, rk, rv):
    B, L, _ = rq.shape
    C = RET_CHUNK
    dm, rd, kd, gc = _retention_tables(C)

    def full(a):
        return pl.BlockSpec(a.shape, lambda b, c: (0,) * a.ndim)

    return pl.pallas_call(
        _ret_prompt_kernel,
        out_shape=[jax.ShapeDtypeStruct((B, L, RET_V), F32),
                   jax.ShapeDtypeStruct((B, RET_HEADS, RET_DK, RET_DV), F32)],
        grid=(B, L // C),
        in_specs=[pl.BlockSpec((1, C, RET_QK), lambda b, c: (b, c, 0)),
                  pl.BlockSpec((1, C, RET_QK), lambda b, c: (b, c, 0)),
                  pl.BlockSpec((1, C, RET_V), lambda b, c: (b, c, 0)),
                  full(dm), full(rd), full(kd), full(gc)],
        out_specs=[pl.BlockSpec((1, C, RET_V), lambda b, c: (b, c, 0)),
                   pl.BlockSpec((1, RET_HEADS, RET_DK, RET_DV), lambda b, c: (b, 0, 0, 0))],
        scratch_shapes=[pltpu.VMEM((RET_HEADS, RET_DK, RET_DV), F32)],
        compiler_params=_params("arbitrary", "arbitrary"), name="retention_prompt",
    )(rq, rk, rv, dm, rd, kd, gc)


def _ret_sample_kernel(q_ref, k_ref, v_ref, st_ref, rd_ref, gc_ref, o_ref, s_ref):
    G = q_ref.shape[0]
    for g in range(G):
        outs = []
        for h in range(RET_HEADS):
            qc = q_ref[g, h]
            kc = k_ref[g, h]
            vr = v_ref[g, h]
            st = st_ref[0, g, h]
            qk = jnp.sum(qc * kc, axis=0, keepdims=True)
            o = qk * vr + jnp.sum(qc * st, axis=0, keepdims=True) * rd_ref[h]
            s_ref[g, h] = gc_ref[h] * st + kc * vr
            outs.append(_norm_rows(o))
        o_ref[pl.ds(g, 1), :] = jnp.concatenate(outs, axis=1)


def sample_retention(layer, rq, rk, rv, state_ret):
    DB = rq.shape[0]
    G = 8
    _, rd, _, gc = _retention_tables(1)
    qT = rq.reshape(DB, RET_HEADS, RET_DK, 1)
    kT = rk.reshape(DB, RET_HEADS, RET_DK, 1)
    v4 = rv.reshape(DB, RET_HEADS, 1, RET_DV)
    col = pl.BlockSpec((G, RET_HEADS, RET_DK, 1), lambda i: (i, 0, 0, 0))
    return pl.pallas_call(
        _ret_sample_kernel,
        out_shape=[jax.ShapeDtypeStruct((DB, RET_V), F32),
                   jax.ShapeDtypeStruct((DB, RET_HEADS, RET_DK, RET_DV), F32)],
        grid=(DB // G,),
        in_specs=[col, col,
                  pl.BlockSpec((G, RET_HEADS, 1, RET_DV), lambda i: (i, 0, 0, 0)),
                  pl.BlockSpec((1, G, RET_HEADS, RET_DK, RET_DV), lambda i: (layer, i, 0, 0, 0)),
                  pl.BlockSpec(rd.shape, lambda i: (0, 0, 0)),
                  pl.BlockSpec(gc.shape, lambda i: (0, 0, 0))],
        out_specs=[pl.BlockSpec((G, RET_V), lambda i: (i, 0)),
                   pl.BlockSpec((G, RET_HEADS, RET_DK, RET_DV), lambda i: (i, 0, 0, 0))],
        compiler_params=_params("arbitrary"), name="retention_sample",
    )(qT, kT, v4, state_ret, rd, gc)


_HALO = 32


def _conv_post(y, g_ref, b_ref):
    return _silu(_norm_rows(y) * g_ref[...] + b_ref[...])


def _conv_prompt_kernel(z_ref, dw_ref, db_ref, g_ref, b_ref, o_ref, zp_scr, *, tl, rs):
    @pl.when(pl.program_id(1) == 0)
    def _():
        zp_scr[0:_HALO, :] = jnp.zeros((_HALO, CONV_CH), F32)

    zp_scr[_HALO:_HALO + tl, :] = z_ref[0]
    first = _HALO - (CONV_W - 1)
    for r in range(tl // rs):
        acc = jnp.broadcast_to(db_ref[...], (rs, CONV_CH))
        for k in range(CONV_W):
            lo = first + k + r * rs
            acc = acc + zp_scr[lo:lo + rs, :] * dw_ref[k:k + 1, :]
        o_ref[0, r * rs:(r + 1) * rs, :] = _conv_post(acc, g_ref, b_ref).astype(o_ref.dtype)
    zp_scr[0:_HALO, :] = zp_scr[tl:tl + _HALO, :]


def prompt_conv(z, dw, db, ln_g, ln_b):
    B, L, _ = z.shape
    tl = min(512, L)
    rs = min(128, tl)
    vec = pl.BlockSpec((1, CONV_CH), lambda b, i: (0, 0))
    return pl.pallas_call(
        functools.partial(_conv_prompt_kernel, tl=tl, rs=rs),
        out_shape=jax.ShapeDtypeStruct((B, L, CONV_CH), BF16),
        grid=(B, L // tl),
        in_specs=[pl.BlockSpec((1, tl, CONV_CH), lambda b, i: (b, i, 0)),
                  pl.BlockSpec((CONV_W, CONV_CH), lambda b, i: (0, 0)), vec, vec, vec],
        out_specs=pl.BlockSpec((1, tl, CONV_CH), lambda b, i: (b, i, 0)),
        scratch_shapes=[pltpu.VMEM((_HALO + tl, CONV_CH), F32)],
        compiler_params=_params("arbitrary", "arbitrary"), name="conv_prompt",
    )(z, dw, db.reshape(1, -1), ln_g.reshape(1, -1), ln_b.reshape(1, -1))


def _conv_sample_kernel(z_ref, buf_ref, dw_ref, db_ref, g_ref, b_ref, o_ref, nb_ref):
    G = z_ref.shape[0]
    w_past = dw_ref[0:CONV_W - 1, :]
    w_last = dw_ref[CONV_W - 1:CONV_W, :]
    rows = []
    for g in range(G):
        buf = buf_ref[0, g]
        zr = z_ref[pl.ds(g, 1), :]
        rows.append(jnp.sum(buf * w_past, axis=0, keepdims=True) + zr * w_last + db_ref[...])
        nb_ref[g, 0:CONV_W - 2, :] = buf_ref[0, g, 1:CONV_W - 1, :]
        nb_ref[g, CONV_W - 2:CONV_W - 1, :] = zr
    o_ref[...] = _conv_post(jnp.concatenate(rows, axis=0), g_ref, b_ref).astype(o_ref.dtype)


def sample_conv(layer, z, state_conv, dw, db, ln_g, ln_b):
    DB = z.shape[0]
    G = 8
    vec = pl.BlockSpec((1, CONV_CH), lambda i: (0, 0))
    return pl.pallas_call(
        _conv_sample_kernel,
        out_shape=[jax.ShapeDtypeStruct((DB, CONV_CH), BF16),
                   jax.ShapeDtypeStruct((DB, CONV_W - 1, CONV_CH), F32)],
        grid=(DB // G,),
        in_specs=[pl.BlockSpec((G, CONV_CH), lambda i: (i, 0)),
                  pl.BlockSpec((1, G, CONV_W - 1, CONV_CH), lambda i: (layer, i, 0, 0)),
                  pl.BlockSpec((CONV_W, CONV_CH), lambda i: (0, 0)), vec, vec, vec],
        out_specs=[pl.BlockSpec((G, CONV_CH), lambda i: (i, 0)),
                   pl.BlockSpec((G, CONV_W - 1, CONV_CH), lambda i: (i, 0, 0))],
        compiler_params=_params("arbitrary"), name="conv_sample",
    )(z, state_conv, dw, db.reshape(1, -1), ln_g.reshape(1, -1), ln_b.reshape(1, -1))


def _merge_kernel(x_ref, oret_ref, rg_ref, act_ref, att_ref, gs_ref, g1_ref, sc2_ref, sh2_ref,
                  wr_ref, wc_ref, bc_ref, wa_ref, wo_ref, lg_ref, lb_ref, *rest):
    x1_ref, h_ref = rest[-2], rest[-1]
    D = D_MODEL
    y_ret = _dot((_silu(rg_ref[0]) * oret_ref[0]).astype(BF16), wr_ref[...])
    y_conv = _dot(act_ref[0], wc_ref[...]) + bc_ref[...]
    y_att = _dot(att_ref[0], wa_ref[...])
    m = gs_ref[0, :, 0:D] * y_ret + gs_ref[0, :, D:2 * D] * y_conv + gs_ref[0, :, 2 * D:3 * D] * y_att
    mix = _dot(m.astype(BF16), wo_ref[...])
    x1 = _norm_rows(ALPHA * x_ref[0] + g1_ref[0] * mix) * lg_ref[...] + lb_ref[...]
    x1_ref[0] = x1
    h_ref[...] = x1 * (1.0 + sc2_ref[0]) + sh2_ref[0]


def merge_and_norm(x, oret, rg, act, att, gsig, g1, sc2, sh2, w_ret, w_conv, b_conv, w_att, w_o, ln_g, ln_b,
                   h_all, n_all, row0, tm):
    B, L, D = x.shape
    nl = L // tm
    blk0 = row0 // tm

    def rows(width):
        return pl.BlockSpec((1, tm, width), lambda b, i: (b, i, 0))

    def full(a):
        return pl.BlockSpec(a.shape, lambda b, i: (0,) * a.ndim)

    vec = pl.BlockSpec((1, D), lambda b, i: (0, 0))
    h_spec = pl.BlockSpec((tm, D), lambda b, i: (blk0 + b * nl + i, 0))
    ms = _mod_spec(g1, tm)
    args = [x, oret, rg, act, att, gsig, g1, sc2, sh2, w_ret, w_conv, b_conv.reshape(1, D), w_att, w_o,
            ln_g.reshape(1, D), ln_b.reshape(1, D)]
    in_specs = [rows(D), rows(RET_V), rows(RET_V), rows(CONV_CH), rows(ATT_W), rows(N_BRANCH * D), ms, ms, ms,
                full(w_ret), full(w_conv), vec, full(w_att), full(w_o), vec, vec]
    aliases = {}
    if h_all is not None:
        args.append(h_all)
        in_specs.append(pl.BlockSpec(memory_space=pl.ANY))
        aliases = {len(args) - 1: 1}
    return pl.pallas_call(
        _merge_kernel,
        out_shape=[jax.ShapeDtypeStruct((B, L, D), F32), jax.ShapeDtypeStruct((n_all, D), F32)],
        grid=(B, nl), in_specs=in_specs,
        out_specs=[rows(D), h_spec],
        input_output_aliases=aliases,
        compiler_params=_params("arbitrary", "arbitrary"), name="merge_norm",
    )(*args)


def _router_kernel(h_ref, whi_ref, wlo_ref, b_ref, e_ref, g_ref):
    h = h_ref[...]
    h_hi = h.astype(BF16)
    h_lo = (h - h_hi.astype(F32)).astype(BF16)
    logits = (_dot(h_hi, whi_ref[...]) + (_dot(h_hi, wlo_ref[...]) + _dot(h_lo, whi_ref[...]))) + b_ref[...]
    lane = lax.broadcasted_iota(I32, logits.shape, 1).astype(F32)
    vals = logits
    e_out = jnp.zeros(logits.shape, F32)
    top = []
    for k in range(MOE_TOP_K):
        m = jnp.max(vals, axis=1, keepdims=True)
        idx = jnp.min(jnp.where(vals == m, lane, float(LANES)), axis=1, keepdims=True)
        e_out = jnp.where(lane == float(k), idx, e_out)
        vals = jnp.where(lane == idx, -jnp.inf, vals)
        top.append(m)
    ex = [jnp.exp(t - top[0]) for t in top]
    den = ex[0] + ex[1] + ex[2] + ex[3]
    g_out = jnp.zeros(logits.shape, F32)
    for k in range(MOE_TOP_K):
        g_out = jnp.where(lane == float(k), ex[k] / den, g_out)
    e_ref[...] = e_out.astype(I32)
    g_ref[...] = g_out


def router_top4(h_all, router_w, router_b, tm):
    N, D = h_all.shape
    wp = jnp.zeros((D, LANES), F32).at[:, :N_EXPERTS].set(router_w)
    w_hi = wp.astype(BF16)
    w_lo = (wp - w_hi.astype(F32)).astype(BF16)
    bp = jnp.full((1, LANES), NEG_BIG, F32).at[0, :N_EXPERTS].set(router_b)
    e, g = pl.pallas_call(
        _router_kernel,
        out_shape=[jax.ShapeDtypeStruct((N, LANES), I32), jax.ShapeDtypeStruct((N, LANES), F32)],
        grid=(N // tm,),
        in_specs=[pl.BlockSpec((tm, D), lambda i: (i, 0)),
                  pl.BlockSpec((D, LANES), lambda i: (0, 0)),
                  pl.BlockSpec((D, LANES), lambda i: (0, 0)),
                  pl.BlockSpec((1, LANES), lambda i: (0, 0))],
        out_specs=[pl.BlockSpec((tm, LANES), lambda i: (i, 0)), pl.BlockSpec((tm, LANES), lambda i: (i, 0))],
        compiler_params=_params("arbitrary"), name="router_top4",
    )(h_all, w_hi, w_lo, bp)
    return e[:, :MOE_TOP_K], g[:, :MOE_TOP_K]


def _moe_kernel(blk_e_ref, nreal_ref, idx_hbm, h_hbm, g_ref, wgu_ref, bgu_ref, wd_ref, bd_ref, y_hbm,
                idx_smem, xbuf, obuf, hid_scr, wgu_bf, wd_bf, sem_idx, sem_g, sem_s, *, bm, nc):
    i = pl.program_id(0)
    n = pl.num_programs(0)

    def idx_copy(c):
        slot = c % 3
        return pltpu.make_async_copy(idx_hbm.at[c], idx_smem.at[pl.ds(slot * 2 * bm, 2 * bm)], sem_idx.at[slot])

    def issue_gather(c):
        base = (c % 3) * 2 * bm
        buf = c % 2

        @pl.when(nreal_ref[c] > 0)
        def _():
            dst = xbuf.at[buf]
            for r in range(bm):
                tok = idx_smem[base + r]
                pltpu.make_async_copy(h_hbm.at[pl.ds(tok, 1)], dst.at[pl.ds(r, 1)], sem_g.at[buf]).start()

    def wait_gather(c):
        buf = c % 2

        @pl.when(nreal_ref[c] > 0)
        def _():
            pltpu.make_async_copy(h_hbm.at[pl.ds(0, bm)], xbuf.at[buf], sem_g.at[buf]).wait()

    def wait_scatter(c):
        @pl.when(nreal_ref[c] > 0)
        def _():
            pltpu.make_async_copy(obuf, y_hbm.at[pl.ds(0, bm)], sem_s).wait()

    @pl.when(i == 0)
    def _():
        idx_copy(0).start()
        idx_copy(0).wait()
        issue_gather(0)

        @pl.when(n > 1)
        def _():
            idx_copy(1).start()

    @pl.when(i + 1 < n)
    def _():
        idx_copy(i + 1).wait()
        issue_gather(i + 1)

        @pl.when(i + 2 < n)
        def _():
            idx_copy(i + 2).start()

    nr = nreal_ref[i]
    buf = i % 2
    wait_gather(i)

    @pl.when((i == 0) | (blk_e_ref[i] != blk_e_ref[jnp.maximum(i - 1, 0)]))
    def _():
        wgu_bf[...] = wgu_ref[0].astype(BF16)
        wd_bf[...] = wd_ref[0].astype(BF16)

    @pl.when(i > 0)
    def _():
        wait_scatter(jnp.maximum(i - 1, 0))

    @pl.when(nr > 0)
    def _():
        x = xbuf[buf].astype(BF16)
        for j in range(D_EXPERT // nc):
            lo = j * nc
            a = _dot(x, wgu_bf[:, lo:lo + nc]) + bgu_ref[0, :, lo:lo + nc]
            lin = _dot(x, wgu_bf[:, D_EXPERT + lo:D_EXPERT + lo + nc]) + bgu_ref[0, :, D_EXPERT + lo:D_EXPERT + lo + nc]
            a = jnp.minimum(a, SWIGLU_LIMIT)
            lin = jnp.clip(lin, -SWIGLU_LIMIT, SWIGLU_LIMIT)
            hid_scr[:, lo:lo + nc] = (a * jax.nn.sigmoid(SWIGLU_ALPHA * a) * (lin + 1.0)).astype(BF16)
        obuf[...] = (_dot(hid_scr[...], wd_bf[...]) + bd_ref[0]) * g_ref[0]

        base = (i % 3) * 2 * bm + bm
        for r in range(bm):
            dst = idx_smem[base + r]
            pltpu.make_async_copy(obuf.at[pl.ds(r, 1)], y_hbm.at[pl.ds(dst, 1)], sem_s).start()

    @pl.when(i == n - 1)
    def _():
        wait_scatter(i)


def moe_dispatch(top_e, gate, bm):
    n_tok = top_e.shape[0]
    n_assign = n_tok * MOE_TOP_K
    flat_e = top_e.reshape(-1)
    order = jnp.argsort(flat_e).astype(I32)
    se = flat_e[order]
    stok = order // MOE_TOP_K
    sk = order % MOE_TOP_K
    sg = gate.reshape(-1)[order]
    counts = jnp.sum((flat_e[:, None] == jnp.arange(N_EXPERTS, dtype=I32)[None, :]).astype(I32), axis=0)
    starts = jnp.cumsum(counts) - counts
    rank = jnp.arange(n_assign, dtype=I32) - starts[se]
    pcounts = (counts + bm - 1) // bm * bm
    pends = jnp.cumsum(pcounts)
    pstarts = pends - pcounts
    ppos = pstarts[se] + rank
    n_chunks = -(-n_assign // bm) + N_EXPERTS
    n_slots = n_chunks * bm
    spare = n_assign + jnp.arange(n_slots, dtype=I32) % bm
    init = jnp.stack([spare, jnp.zeros((n_slots,), I32)], axis=1)
    vals = jnp.stack([sk * n_tok + stok, lax.bitcast_convert_type(sg, I32)], axis=1)
    packed = init.at[ppos].set(vals)
    buf_dst = packed[:, 0]
    buf_tok = jnp.where(buf_dst < n_assign, buf_dst % n_tok, 0)
    buf_g = lax.bitcast_convert_type(packed[:, 1], F32)
    chunk0 = jnp.arange(n_chunks, dtype=I32) * bm
    blk_e = jnp.minimum(jnp.sum((chunk0[:, None] >= pends[None, :]).astype(I32), axis=1), N_EXPERTS - 1)
    nreal = jnp.clip(pstarts[blk_e] + counts[blk_e] - chunk0, 0, bm).astype(I32)
    idx = jnp.concatenate([buf_tok.reshape(n_chunks, bm), buf_dst.reshape(n_chunks, bm)], axis=1)
    return blk_e, nreal, idx, buf_g.reshape(n_chunks, bm, 1)


def moe_experts(h_all, top_e, gate, layer, exp_w_gu, exp_b_gu, exp_w_down, exp_b_down):
    blk_e, nreal, idx, gcol = moe_dispatch(top_e, gate, MOE_BM)
    return moe_grouped_matmul(blk_e, nreal, idx, gcol, h_all, layer, exp_w_gu, exp_b_gu, exp_w_down, exp_b_down)


def moe_grouped_matmul(blk_e, nreal, idx, gcol, h_all, layer, exp_w_gu, exp_b_gu, exp_w_down, exp_b_down):
    N, D = h_all.shape
    bm = MOE_BM
    nc = 256
    n_chunks = idx.shape[0]
    b_gu = exp_b_gu.reshape(DEPTH, N_EXPERTS, 1, 2 * D_EXPERT)
    b_dn = exp_b_down.reshape(DEPTH, N_EXPERTS, 1, D)
    grid_spec = pltpu.PrefetchScalarGridSpec(
        num_scalar_prefetch=2, grid=(n_chunks,),
        in_specs=[pl.BlockSpec(memory_space=pl.ANY),
                  pl.BlockSpec(memory_space=pl.ANY),
                  pl.BlockSpec((1, bm, 1), lambda i, be, nr: (i, 0, 0)),
                  pl.BlockSpec((1, 1, D, 2 * D_EXPERT), lambda i, be, nr: (layer, be[i], 0, 0)),
                  pl.BlockSpec((1, 1, 1, 2 * D_EXPERT), lambda i, be, nr: (layer, be[i], 0, 0)),
                  pl.BlockSpec((1, 1, D_EXPERT, D), lambda i, be, nr: (layer, be[i], 0, 0)),
                  pl.BlockSpec((1, 1, 1, D), lambda i, be, nr: (layer, be[i], 0, 0))],
        out_specs=pl.BlockSpec(memory_space=pl.ANY),
        scratch_shapes=[pltpu.SMEM((3 * 2 * bm,), I32),
                        pltpu.VMEM((2, bm, D), F32),
                        pltpu.VMEM((bm, D), F32),
                        pltpu.VMEM((bm, D_EXPERT), BF16),
                        pltpu.VMEM((D, 2 * D_EXPERT), BF16),
                        pltpu.VMEM((D_EXPERT, D), BF16),
                        pltpu.SemaphoreType.DMA((3,)),
                        pltpu.SemaphoreType.DMA((2,)),
                        pltpu.SemaphoreType.DMA(())])
    y = pl.pallas_call(
        functools.partial(_moe_squeeze_kernel, bm=bm, nc=nc),
        out_shape=jax.ShapeDtypeStruct((MOE_TOP_K * N + bm, D), F32),
        grid_spec=grid_spec,
        compiler_params=_params("arbitrary"), name="moe_experts",
    )(blk_e, nreal, idx, h_all, gcol, exp_w_gu, b_gu, exp_w_down, b_dn)
    return y


def _moe_squeeze_kernel(blk_e_ref, nreal_ref, idx_hbm, h_hbm, g_ref, wgu_ref, bgu_ref, wd_ref, bd_ref, y_hbm,
                        *scratch, bm, nc):
    _moe_kernel(blk_e_ref, nreal_ref, idx_hbm, h_hbm, g_ref, wgu_ref.at[0], bgu_ref.at[0], wd_ref.at[0],
                bd_ref.at[0], y_hbm, *scratch, bm=bm, nc=nc)


def _combine_kernel(x_ref, y0_ref, y1_ref, y2_ref, y3_ref, g2_ref, lg_ref, lb_ref, o_ref):
    ff = (y0_ref[...] + y1_ref[...]) + (y2_ref[...] + y3_ref[...])
    o_ref[0] = _norm_rows(ALPHA * x_ref[0] + g2_ref[0] * ff) * lg_ref[...] + lb_ref[...]


def combine_and_norm(x1, y, n_all, g2, ln_g, ln_b, row0, tm):
    B, L, D = x1.shape
    nl = L // tm
    blk0 = row0 // tm
    per_k = n_all // tm
    vec = pl.BlockSpec((1, D), lambda b, i: (0, 0))

    def y_spec(k):
        return pl.BlockSpec((tm, D), lambda b, i, k=k: (k * per_k + blk0 + b * nl + i, 0))

    return pl.pallas_call(
        _combine_kernel,
        out_shape=jax.ShapeDtypeStruct((B, L, D), F32),
        grid=(B, nl),
        in_specs=[pl.BlockSpec((1, tm, D), lambda b, i: (b, i, 0))] + [y_spec(k) for k in range(MOE_TOP_K)]
        + [_mod_spec(g2, tm), vec, vec],
        out_specs=pl.BlockSpec((1, tm, D), lambda b, i: (b, i, 0)),
        compiler_params=_params("arbitrary", "arbitrary"), name="combine_norm",
    )(x1, y, y, y, y, g2, ln_g.reshape(1, D), ln_b.reshape(1, D))


def _rope_tables(pos):
    half = HEAD_DIM // 2
    inv = 1.0 / (ROPE_THETA ** (jnp.arange(half, dtype=F32) / half))
    ang = pos.astype(F32)[:, None] * inv[None, :]
    cos, sin = jnp.cos(ang), jnp.sin(ang)
    reps = LANES // HEAD_DIM
    return jnp.tile(jnp.concatenate([cos, cos], axis=1), (1, reps)), jnp.tile(jnp.concatenate([-sin, sin], axis=1), (1, reps))


def _split_w_in(w):
    o = IN_OFFSETS
    aq, ak, av, iq, ik, iw, rq, rk, rv, rg, cglu, gl = [w[:, o[i]:o[i + 1]] for i in range(len(IN_SPLITS))]
    D = w.shape[0]
    pad_ik = jnp.zeros((D, _R_END - _R_IK - IDX_DIM), F32)
    pad_iw = jnp.zeros((D, _P_END - _P_IW - IDX_HEADS), F32)
    w_rope = jnp.concatenate([aq * (HEAD_DIM ** -0.5), ak, iq, rq, rk * (RET_DK ** -0.5), ik, pad_ik], axis=1)
    w_plain = jnp.concatenate([av, rv, rg, cglu, iw * ((IDX_DIM ** -0.5) * (IDX_HEADS ** -0.5)), pad_iw], axis=1)
    return w_rope.astype(BF16), w_plain.astype(BF16), gl.astype(BF16)


def kernel(x_prompt, x_sample, c_prompt, c_sample, cache_k, cache_v, cache_idx_k, state_ret, state_conv, page_table, ada_w, ada_b, w_in, w_ret_out, conv_dw, conv_db, conv_ln_g, conv_ln_b, w_conv_out, b_conv_out, w_att_out, w_o, ln1_g, ln1_b, router_w, router_b, exp_w_gu, exp_b_gu, exp_w_down, exp_b_down, ln2_g, ln2_b):
    B, L, D = x_prompt.shape
    DB = x_sample.shape[0]
    n_p = B * L
    n_all = n_p + DB
    past = page_table.shape[1] * PAGE_SIZE

    r_pad = -(B + DB) % 8
    c_all = jnp.concatenate([c_prompt, c_sample, jnp.zeros((r_pad, D), F32)], axis=0)
    mod = ada_modulation(c_all, ada_w, ada_b)

    cos_p, sin_p = _rope_tables(jnp.arange(L, dtype=I32))
    cos_s, sin_s = _rope_tables(jnp.full((DB,), past, I32))

    xp = x_prompt
    xs = x_sample.reshape(1, DB, D)
    new_p = ([], [], [], [], [])
    new_s = ([], [], [], [], [])
    tm_p = min(PROJ_TM, L)
    tm_m = min(256, L)
    tm_r = math.gcd(n_all, 384)
    tm_c = math.gcd(math.gcd(L, DB), 128)

    for l in range(DEPTH):
        mp = [mod[l, :B, i * D:(i + 1) * D].reshape(B, 1, D) for i in range(6)]
        ms = [mod[l, B:B + DB, i * D:(i + 1) * D].reshape(1, DB, D) for i in range(6)]
        w_rope, w_plain, w_gate = _split_w_in(w_in[l])
        wr, wc, wa, wo = (w_ret_out[l].astype(BF16), w_conv_out[l].astype(BF16), w_att_out[l].astype(BF16),
                          w_o[l].astype(BF16))

        pp = input_projection(xp, mp[1], mp[0], cos_p, sin_p, w_rope, w_plain, w_gate, tm_p)
        ps = input_projection(xs, ms[1], ms[0], cos_s, sin_s, w_rope, w_plain, w_gate, DB)

        att_p = prompt_sparse_attention(pp['aq'], pp['iq'], pp['iw'], pp['akt'], pp['avh'], pp['ikt'])
        oret_p, sret_p = prompt_retention(pp['rq'], pp['rk'], pp['rv'])
        act_p = prompt_conv(pp['z'], conv_dw[l], conv_db[l], conv_ln_g[l], conv_ln_b[l])

        aq_s = jnp.transpose(ps['aq'][0], (1, 0, 2)).reshape(DB, ATT_W).astype(F32)
        iq_s = jnp.transpose(ps['iq'][0], (1, 0, 2)).reshape(DB, IDX_HEADS * IDX_DIM)
        att_s = sample_sparse_attention(l, aq_s, iq_s, ps['iw'][0], ps['ak'][0],
                                        ps['av'][0], ps['ik'][0], cache_k, cache_v, cache_idx_k, page_table)
        oret_s, sret_s = sample_retention(l, ps['rq'][0].astype(F32), ps['rk'][0], ps['rv'][0].astype(F32), state_ret)
        act_s, conv_s = sample_conv(l, ps['z'][0], state_conv, conv_dw[l], conv_db[l], conv_ln_g[l], conv_ln_b[l])

        x1p, h_all = merge_and_norm(xp, oret_p, pp['rg'], act_p, att_p, pp['gsig'], mp[2], mp[4], mp[3],
                                    wr, wc, b_conv_out[l], wa, wo, ln1_g[l], ln1_b[l], None, n_all, 0, tm_m)
        x1s, h_all = merge_and_norm(xs, oret_s.reshape(1, DB, -1), ps['rg'], act_s.reshape(1, DB, -1),
                                    att_s.reshape(1, DB, -1), ps['gsig'], ms[2], ms[4], ms[3],
                                    wr, wc, b_conv_out[l], wa, wo, ln1_g[l], ln1_b[l], h_all, n_all, n_p, DB)

        top_e, gate = router_top4(h_all, router_w[l], router_b[l], tm_r)
        y4 = moe_experts(h_all, top_e, gate, l, exp_w_gu, exp_b_gu, exp_w_down, exp_b_down)
        xp = combine_and_norm(x1p, y4, n_all, mp[5], ln2_g[l], ln2_b[l], 0, tm_c)
        xs = combine_and_norm(x1s, y4, n_all, ms[5], ln2_g[l], ln2_b[l], n_p, tm_c)

        for lst, a in zip(new_p, (pp['ak'].reshape(B, L, N_HEADS, HEAD_DIM), pp['av'].reshape(B, L, N_HEADS, HEAD_DIM),
                                  pp['ik'], sret_p, pp['z'][:, L - (CONV_W - 1):, :])):
            lst.append(a)
        for lst, a in zip(new_s, (ps['ak'].reshape(DB, 1, N_HEADS, HEAD_DIM), ps['av'].reshape(DB, 1, N_HEADS, HEAD_DIM),
                                  ps['ik'].reshape(DB, 1, IDX_DIM), sret_s, conv_s)):
            lst.append(a)

    nk_p, nv_p, nik_p, nr_p, nc_p = [jnp.stack(a) for a in new_p]
    nk_s, nv_s, nik_s, nr_s, nc_s = [jnp.stack(a) for a in new_s]
    return (xp, xs.reshape(DB, 1, D), nk_p, nv_p, nik_p, nr_p, nc_p, nk_s, nv_s, nik_s, nr_s, nc_s)
```

```python
import functools
import math

import numpy as np
import jax
import jax.numpy as jnp
from jax import lax
from jax.experimental import pallas as pl
from jax.experimental.pallas import tpu as pltpu

F32 = jnp.float32
BF16 = jnp.bfloat16
I32 = jnp.int32

D_MODEL = 1024
DEPTH = 4
PAGE_SIZE = 128
N_HEADS = 8
HEAD_DIM = 64
ATT_W = N_HEADS * HEAD_DIM
IDX_HEADS = 4
IDX_DIM = 64
TOPK_MAX = 256
Q_BLOCK = 128
ROPE_THETA = 10000.0
RET_HEADS = 4
RET_DK = 64
RET_DV = 128
RET_QK = RET_HEADS * RET_DK
RET_V = RET_HEADS * RET_DV
RET_CHUNK = 128
CONV_CH = 512
CONV_W = 31
N_EXPERTS = 32
MOE_TOP_K = 4
D_EXPERT = 1024
SWIGLU_LIMIT = 7.0
SWIGLU_ALPHA = 1.702
N_BRANCH = 3
ALPHA = (2 * DEPTH) ** 0.25
LN_EPS = 1e-5
IN_SPLITS = (ATT_W, ATT_W, ATT_W, IDX_HEADS * IDX_DIM, IDX_DIM, IDX_HEADS, RET_QK, RET_QK, RET_V, RET_V,
             2 * CONV_CH, N_BRANCH * D_MODEL)
IN_OFFSETS = tuple(int(o) for o in np.cumsum((0,) + IN_SPLITS))

LANES = 128
INT_MIN = -2 ** 31
NEG_BIG = -1e30
MOE_BM = 256
PROJ_TM = 512
HEAD_GROUP = 4
VMEM_LIMIT = 56 * 1024 * 1024


def _dot(a, b):
    return jnp.dot(a, b, preferred_element_type=F32)


def _dot_nt(a, b):
    return lax.dot_general(a, b, (((1,), (1,)), ((), ())), preferred_element_type=F32)


def _dot_tn(a, b):
    return lax.dot_general(a, b, (((0,), (0,)), ((), ())), preferred_element_type=F32)


def _silu(x):
    return x * jax.nn.sigmoid(x)


def _norm_rows(x):
    mu = jnp.mean(x, axis=-1, keepdims=True)
    xc = x - mu
    var = jnp.mean(xc * xc, axis=-1, keepdims=True)
    return xc * lax.rsqrt(var + LN_EPS)


def _params(*sem):
    return pltpu.CompilerParams(dimension_semantics=sem, vmem_limit_bytes=VMEM_LIMIT)


def _ada_kernel(c_ref, w_ref, b_ref, o_ref):
    c = c_ref[...]
    o_ref[0] = _dot(_silu(c).astype(BF16), w_ref[0].astype(BF16)) + b_ref[0]


def ada_modulation(c_all, ada_w, ada_b):
    R, D = c_all.shape
    W = ada_w.shape[-1]
    tn = 1536
    return pl.pallas_call(
        _ada_kernel,
        out_shape=jax.ShapeDtypeStruct((DEPTH, R, W), F32),
        grid=(DEPTH, W // tn),
        in_specs=[pl.BlockSpec((R, D), lambda l, j: (0, 0)),
                  pl.BlockSpec((1, D, tn), lambda l, j: (l, 0, j)),
                  pl.BlockSpec((1, 1, tn), lambda l, j: (l, 0, j))],
        out_specs=pl.BlockSpec((1, R, tn), lambda l, j: (l, 0, j)),
        compiler_params=_params("arbitrary", "arbitrary"),
        name="ada_modulation",
    )(c_all, ada_w, ada_b.reshape(DEPTH, 1, W))


def _rope_chunks(y, cos, sin, first_half):
    parts = []
    for c in range(y.shape[1] // LANES):
        yc = y[:, c * LANES:(c + 1) * LANES]
        rot = jnp.where(first_half, pltpu.roll(yc, LANES - HEAD_DIM // 2, 1), pltpu.roll(yc, HEAD_DIM // 2, 1))
        parts.append(yc * cos + rot * sin)
    return parts[0] if len(parts) == 1 else jnp.concatenate(parts, axis=1)


_R_AQ, _R_AK, _R_IQ, _R_RQ, _R_RK, _R_IK, _R_END = 0, 512, 1024, 1280, 1536, 1792, 1920


def _proj_rope_kernel(x_ref, sc_ref, sh_ref, cos_ref, sin_ref, w_ref,
                      aq_ref, ak_ref, akt_ref, iq_ref, rq_ref, rk_ref, ik_ref, ikt_ref):
    u = (x_ref[0] * (1.0 + sc_ref[0]) + sh_ref[0]).astype(BF16)
    cos = cos_ref[...]
    sin = sin_ref[...]
    lane = lax.broadcasted_iota(I32, cos.shape, 1)
    first_half = (lane % HEAD_DIM) < (HEAD_DIM // 2)

    def proj(lo, hi):
        return _rope_chunks(_dot(u, w_ref[:, lo:hi]), cos, sin, first_half)

    aq = proj(_R_AQ, _R_AK).astype(BF16)
    for h in range(N_HEADS):
        aq_ref[0, h] = aq[:, h * HEAD_DIM:(h + 1) * HEAD_DIM]
    ak = proj(_R_AK, _R_IQ)
    ak_ref[0] = ak
    akt_ref[0, 0] = ak.T.astype(BF16)
    iq = proj(_R_IQ, _R_RQ).astype(BF16)
    for h in range(IDX_HEADS):
        iq_ref[0, h] = iq[:, h * IDX_DIM:(h + 1) * IDX_DIM]
    rq_ref[0] = proj(_R_RQ, _R_RK).astype(BF16)
    rk_ref[0] = proj(_R_RK, _R_IK)
    ik2 = proj(_R_IK, _R_END)
    ik_ref[0] = ik2[:, :IDX_DIM]
    ikt_ref[0, 0] = ik2.T[:IDX_DIM, :].astype(BF16)


_P_AV, _P_RV, _P_RG, _P_CA, _P_CG, _P_IW, _P_END = 0, 512, 1024, 1536, 2048, 2560, 2688


def _proj_plain_kernel(x_ref, sc_ref, sh_ref, w_ref, av_ref, avh_ref, rv_ref, rg_ref, z_ref, iw_ref):
    u = (x_ref[0] * (1.0 + sc_ref[0]) + sh_ref[0]).astype(BF16)
    av = _dot(u, w_ref[:, _P_AV:_P_RV])
    av_ref[0] = av
    ones = jnp.ones((av.shape[0], LANES - HEAD_DIM), BF16)
    avb = av.astype(BF16)
    for h in range(N_HEADS):
        avh_ref[0, h] = jnp.concatenate([avb[:, h * HEAD_DIM:(h + 1) * HEAD_DIM], ones], axis=1)
    rv_ref[0] = _dot(u, w_ref[:, _P_RV:_P_RG]).astype(BF16)
    rg_ref[0] = _dot(u, w_ref[:, _P_RG:_P_CA])
    a = _dot(u, w_ref[:, _P_CA:_P_CG])
    g = _dot(u, w_ref[:, _P_CG:_P_IW])
    z_ref[0] = a * jax.nn.sigmoid(g)
    iw_ref[0] = _dot(u, w_ref[:, _P_IW:_P_END])


def _proj_gate_kernel(x_ref, sc_ref, sh_ref, w_ref, gs_ref):
    u = (x_ref[0] * (1.0 + sc_ref[0]) + sh_ref[0]).astype(BF16)
    for c in range(N_BRANCH):
        gs_ref[0, :, c * D_MODEL:(c + 1) * D_MODEL] = jax.nn.sigmoid(
            _dot(u, w_ref[:, c * D_MODEL:(c + 1) * D_MODEL]))


def _mod_spec(mod, tm):
    if mod.shape[1] == 1:
        return pl.BlockSpec((1, 1, D_MODEL), lambda b, i: (b, 0, 0))
    return pl.BlockSpec((1, tm, D_MODEL), lambda b, i: (b, i, 0))


def input_projection(x, sc, sh, cos_t, sin_t, w_rope, w_plain, w_gate, tm):
    B, L, D = x.shape
    grid = (B, L // tm)
    xs = pl.BlockSpec((1, tm, D), lambda b, i: (b, i, 0))
    ms = _mod_spec(sc, tm)
    ts = pl.BlockSpec((tm, LANES), lambda b, i: (i, 0))

    def ws(w):
        return pl.BlockSpec(w.shape, lambda b, i: (0, 0))

    def o(width, dt):
        return jax.ShapeDtypeStruct((B, L, width), dt), pl.BlockSpec((1, tm, width), lambda b, i: (b, i, 0))

    nt = L // tm

    def heads(n, width, dt):
        return jax.ShapeDtypeStruct((B, n, L, width), dt), pl.BlockSpec((1, n, tm, width), lambda b, i: (b, 0, i, 0))

    def chunked_t(rows, dt):
        return jax.ShapeDtypeStruct((B, nt, rows, tm), dt), pl.BlockSpec((1, 1, rows, tm), lambda b, i: (b, i, 0, 0))

    outs = [heads(N_HEADS, HEAD_DIM, BF16), o(ATT_W, F32), chunked_t(ATT_W, BF16), heads(IDX_HEADS, IDX_DIM, BF16),
            o(RET_QK, BF16), o(RET_QK, F32), o(IDX_DIM, F32), chunked_t(IDX_DIM, BF16)]
    aq, ak, akt, iq, rq, rk, ik, ikt = pl.pallas_call(
        _proj_rope_kernel,
        out_shape=[s for s, _ in outs], grid=grid,
        in_specs=[xs, ms, ms, ts, ts, ws(w_rope)],
        out_specs=[s for _, s in outs],
        compiler_params=_params("arbitrary", "arbitrary"), name="proj_rope",
    )(x, sc, sh, cos_t, sin_t, w_rope)

    outs = [o(ATT_W, F32), heads(N_HEADS, LANES, BF16), o(RET_V, BF16), o(RET_V, F32), o(CONV_CH, F32), o(LANES, F32)]
    av, avh, rv, rg, z, iw = pl.pallas_call(
        _proj_plain_kernel,
        out_shape=[s for s, _ in outs], grid=grid,
        in_specs=[xs, ms, ms, ws(w_plain)],
        out_specs=[s for _, s in outs],
        compiler_params=_params("arbitrary", "arbitrary"), name="proj_plain",
    )(x, sc, sh, w_plain)

    gs_shape, gs_spec = o(N_BRANCH * D_MODEL, F32)
    gsig = pl.pallas_call(
        _proj_gate_kernel,
        out_shape=gs_shape, grid=grid,
        in_specs=[xs, ms, ms, ws(w_gate)],
        out_specs=gs_spec,
        compiler_params=_params("arbitrary", "arbitrary"), name="proj_gate",
    )(x, sc, sh, w_gate)
    return dict(aq=aq, ak=ak, akt=akt, iq=iq, rq=rq, rk=rk, ik=ik, ikt=ikt,
                av=av, avh=avh, rv=rv, rg=rg, z=z, iw=iw, gsig=gsig)


def _sort_key(score):
    score = jnp.where(score == 0.0, 0.0, score)
    b = pltpu.bitcast(score, I32)
    return b ^ ((b >> 31) & 0x7FFFFFFF)


def _topk_threshold(count_ge, rows, topk):
    k = float(topk)
    t = jnp.where(count_ge(jnp.zeros((rows, 1), I32)) >= k, 0, INT_MIN).astype(I32)

    def bit_step(i, t):
        cand = t | jnp.left_shift(1, 30 - i).astype(I32)
        return jnp.where(count_ge(cand) >= k, cand, t)

    return lax.fori_loop(0, 31, bit_step, t)


def _tie_bound(count_tie_below, t, need, rows, pos_bits):
    def pos_step(i, j):
        cand = j | jnp.left_shift(1, pos_bits - 1 - i).astype(I32)
        return jnp.where(count_tie_below(t, cand) < need, cand, j)

    return lax.fori_loop(0, pos_bits, pos_step, jnp.zeros((rows, 1), I32))


def _dsa_prompt_kernel(q_ref, iq_ref, iw_ref, kt_ref, vh_ref, ikt_ref, o_ref,
                       key_scr, bias_scr, j_scr, m_scr, acc_scr, *, topk, ck, pos_bits):
    qb = q_ref.shape[2]
    t0 = pl.program_id(1) * qb
    nkc = (t0 + qb + ck - 1) // ck
    qpos = t0 + lax.broadcasted_iota(I32, (qb, 1), 0)
    iw = iw_ref[0]
    iw_cols = [iw[:, h:h + 1] for h in range(IDX_HEADS)]

    def chunk_kpos(c):
        return c * ck + lax.broadcasted_iota(I32, (qb, ck), 1)

    def score_chunk(c, carry):
        ikc = ikt_ref[0, c]
        acc = jnp.zeros((qb, ck), F32)
        for h in range(IDX_HEADS):
            acc = acc + jnp.maximum(_dot(iq_ref[0, h], ikc), 0.0) * iw_cols[h]
        key_scr[c] = jnp.where(chunk_kpos(c) <= qpos, _sort_key(acc), INT_MIN)
        return carry

    lax.fori_loop(0, nkc, score_chunk, 0)

    def fold(m):
        s = m[:, :LANES]
        for j in range(1, ck // LANES):
            s = s + m[:, j * LANES:(j + 1) * LANES]
        return s

    def count_ge(cand):
        def body(c, acc):
            return acc + fold((key_scr[c] >= cand).astype(F32))
        return jnp.sum(lax.fori_loop(0, nkc, body, jnp.zeros((qb, LANES), F32)), axis=1, keepdims=True)

    def count_tie_below(t, j):
        def body(c, acc):
            hit = (key_scr[c] == t) & (chunk_kpos(c) < j)
            return acc + fold(hit.astype(F32))
        return jnp.sum(lax.fori_loop(0, nkc, body, jnp.zeros((qb, LANES), F32)), axis=1, keepdims=True)

    t = _topk_threshold(count_ge, qb, topk)
    n_gt = count_ge(t + 1)
    n_tie = count_ge(t) - n_gt
    need = float(topk) - n_gt
    j_scr[...] = jnp.full((qb, 1), 2 ** pos_bits - 1, I32)
    short = jnp.max(jnp.where((n_tie > need) & (t > INT_MIN), 1.0, 0.0))

    @pl.when(short > 0.0)
    def _():
        j_scr[...] = _tie_bound(count_tie_below, t, need, qb, pos_bits)

    j = j_scr[...]

    def bias_chunk(c, carry):
        kpos = chunk_kpos(c)
        key = key_scr[c]
        sel = ((key > t) | ((key == t) & (kpos <= j))) & (kpos <= qpos)
        bias_scr[c] = jnp.where(sel, 0.0, NEG_BIG)
        return carry

    lax.fori_loop(0, nkc, bias_chunk, 0)

    m_scr[...] = jnp.full(m_scr.shape, NEG_BIG, F32)
    acc_scr[...] = jnp.zeros(acc_scr.shape, F32)

    def att_chunk(c, carry):
        off = pl.multiple_of(c * ck, ck)
        bias = bias_scr[c]
        for g in range(0, N_HEADS, HEAD_GROUP):
            hs = range(g, g + HEAD_GROUP)
            s = [_dot(q_ref[0, h], kt_ref[0, c, h * HEAD_DIM:(h + 1) * HEAD_DIM, :]) + bias for h in hs]
            m_prev = [m_scr[h] for h in hs]
            m_new = [jnp.maximum(mp, jnp.max(si, axis=1, keepdims=True)) for mp, si in zip(m_prev, s)]
            p = [jnp.exp(si - mn).astype(BF16) for si, mn in zip(s, m_new)]
            pv = [_dot(pi, vh_ref[0, h, pl.ds(off, ck), :]) for pi, h in zip(p, hs)]
            for i, h in enumerate(hs):
                acc_scr[h] = jnp.exp(m_prev[i] - m_new[i]) * acc_scr[h] + pv[i]
                m_scr[h] = m_new[i]
        return carry

    lax.fori_loop(0, nkc, att_chunk, 0)
    outs = []
    for h in range(N_HEADS):
        acc = acc_scr[h]
        outs.append(acc[:, :HEAD_DIM] / acc[:, HEAD_DIM:2 * HEAD_DIM])
    o_ref[0] = jnp.concatenate(outs, axis=1).astype(o_ref.dtype)


def prompt_sparse_attention(aq, iq, iw, akt, avh, ikt):
    B, _, L, _ = aq.shape
    nc, _, ck = akt.shape[1:]
    topk = min(TOPK_MAX, L // 4)
    qb = Q_BLOCK
    kern = functools.partial(_dsa_prompt_kernel, topk=topk, ck=ck, pos_bits=max(1, (L - 1).bit_length()))
    return pl.pallas_call(
        kern,
        out_shape=jax.ShapeDtypeStruct((B, L, ATT_W), BF16),
        grid=(B, L // qb),
        in_specs=[pl.BlockSpec((1, N_HEADS, qb, HEAD_DIM), lambda b, i: (b, 0, i, 0)),
                  pl.BlockSpec((1, IDX_HEADS, qb, IDX_DIM), lambda b, i: (b, 0, i, 0)),
                  pl.BlockSpec((1, qb, LANES), lambda b, i: (b, i, 0)),
                  pl.BlockSpec((1, nc, ATT_W, ck), lambda b, i: (b, 0, 0, 0)),
                  pl.BlockSpec((1, N_HEADS, L, LANES), lambda b, i: (b, 0, 0, 0)),
                  pl.BlockSpec((1, nc, IDX_DIM, ck), lambda b, i: (b, 0, 0, 0))],
        out_specs=pl.BlockSpec((1, qb, ATT_W), lambda b, i: (b, i, 0)),
        scratch_shapes=[pltpu.VMEM((nc, qb, ck), I32), pltpu.VMEM((nc, qb, ck), F32),
                        pltpu.VMEM((qb, 1), I32),
                        pltpu.VMEM((N_HEADS, qb, 1), F32),
                        pltpu.VMEM((N_HEADS, qb, LANES), F32)],
        compiler_params=_params("arbitrary", "arbitrary"), name="dsa_prompt",
    )(aq, iq, iw, akt, avh, ikt)


def _dsa_sample_kernel(pt_ref, q_ref, iq_ref, iw_ref, kn_ref, vn_ref, ikn_ref, *rest, n_pages, topk, pos_bits):
    ik_pages = rest[:n_pages]
    k_pages = rest[n_pages:2 * n_pages]
    v_pages = rest[2 * n_pages:3 * n_pages]
    o_ref = rest[3 * n_pages]
    past = n_pages * PAGE_SIZE
    nk = past + LANES
    rows = iq_ref.shape[1]

    iq = iq_ref[0]
    iwc = iw_ref[0]
    parts = []
    for p in range(n_pages):
        s = _dot(iq, ik_pages[p][0, 0].astype(BF16))
        parts.append(jnp.sum(jnp.maximum(s, 0.0) * iwc, axis=0, keepdims=True))
    ikn = ikn_ref[0].astype(BF16).astype(F32)
    s_new = jnp.sum(iq.astype(F32) * ikn, axis=1, keepdims=True)
    s_new = jnp.sum(jnp.maximum(s_new, 0.0) * iwc, axis=0, keepdims=True)
    parts.append(jnp.broadcast_to(s_new, (1, LANES)))
    score = jnp.concatenate(parts, axis=1)
    kpos = lax.broadcasted_iota(I32, (1, nk), 1)
    admissible = kpos <= past
    key = jnp.where(admissible, _sort_key(score), INT_MIN)

    def count_ge(cand):
        return jnp.sum((key >= cand).astype(F32), axis=1, keepdims=True)

    def count_tie_below(t, j):
        return jnp.sum(((key == t) & (kpos < j)).astype(F32), axis=1, keepdims=True)

    t = _topk_threshold(count_ge, 1, topk)
    j = _tie_bound(count_tie_below, t, float(topk) - count_ge(t + 1), 1, pos_bits)
    sel = ((key > t) | ((key == t) & (kpos <= j))) & admissible
    bias = jnp.where(sel, 0.0, NEG_BIG)

    hrow = lax.broadcasted_iota(I32, (N_HEADS, ATT_W), 0)
    hcol = lax.broadcasted_iota(I32, (N_HEADS, ATT_W), 1) // HEAD_DIM
    diag = hrow == hcol
    qbd = jnp.where(diag, jnp.broadcast_to(q_ref[0], (N_HEADS, ATT_W)), 0.0)
    qbd_b = qbd.astype(BF16)
    def page_t(ref):
        return ref[0, 0].reshape(ATT_W, PAGE_SIZE).astype(BF16)

    sp = [_dot(qbd_b, page_t(k_pages[p])) for p in range(n_pages)]
    kn = kn_ref[0].astype(BF16).astype(F32)
    s_n = jnp.sum(qbd_b.astype(F32) * kn, axis=1, keepdims=True)
    sp.append(jnp.broadcast_to(s_n, (N_HEADS, LANES)))
    s = jnp.concatenate(sp, axis=1) + bias
    m = jnp.max(s, axis=1, keepdims=True)
    p_all = jnp.exp(s - m)
    l = jnp.sum(p_all, axis=1, keepdims=True)
    acc = jnp.zeros((N_HEADS, ATT_W), F32)
    for p in range(n_pages):
        acc = acc + _dot_nt(p_all[:, p * PAGE_SIZE:(p + 1) * PAGE_SIZE].astype(BF16), page_t(v_pages[p]))
    vn = vn_ref[0].astype(BF16).astype(F32)
    p_new = p_all[:, past:past + 1].astype(BF16).astype(F32)
    acc = acc + p_new * vn
    o = jnp.where(diag, acc / l, 0.0)
    o_ref[0] = jnp.sum(o, axis=0, keepdims=True).astype(o_ref.dtype)


def sample_sparse_attention(layer, aq, iq, iw, ak, av, ik, cache_k, cache_v, cache_idx_k, page_table):
    DB = aq.shape[0]
    n_pages = page_table.shape[1]
    past = n_pages * PAGE_SIZE
    topk = min(TOPK_MAX, (past + 1) // 4)
    rows = 8
    iq8 = jnp.pad(iq.reshape(DB, IDX_HEADS, IDX_DIM), ((0, 0), (0, rows - IDX_HEADS), (0, 0)))
    iw8 = jnp.pad(iw[:, :IDX_HEADS], ((0, 0), (0, rows - IDX_HEADS))).reshape(DB, rows, 1)

    def row_spec(width):
        return pl.BlockSpec((1, 1, width), lambda b, pt: (b, 0, 0))

    ik_t = jnp.transpose(cache_idx_k, (0, 1, 3, 2))
    ck_t = jnp.transpose(cache_k, (0, 1, 3, 4, 2))
    cv_t = jnp.transpose(cache_v, (0, 1, 3, 4, 2))

    def ik_page_spec(p):
        return pl.BlockSpec((1, 1, IDX_DIM, PAGE_SIZE), lambda b, pt, p=p: (layer, pt[b, p], 0, 0))

    def kv_page_spec(p):
        return pl.BlockSpec((1, 1, N_HEADS, HEAD_DIM, PAGE_SIZE), lambda b, pt, p=p: (layer, pt[b, p], 0, 0, 0))

    in_specs = [row_spec(ATT_W),
                pl.BlockSpec((1, rows, IDX_DIM), lambda b, pt: (b, 0, 0)),
                pl.BlockSpec((1, rows, 1), lambda b, pt: (b, 0, 0)),
                row_spec(ATT_W), row_spec(ATT_W), row_spec(IDX_DIM)]
    in_specs += [ik_page_spec(p) for p in range(n_pages)]
    in_specs += [kv_page_spec(p) for p in range(n_pages)]
    in_specs += [kv_page_spec(p) for p in range(n_pages)]
    kern = functools.partial(_dsa_sample_kernel, n_pages=n_pages, topk=topk,
                             pos_bits=max(1, (past + LANES - 1).bit_length()))
    out = pl.pallas_call(
        kern,
        out_shape=jax.ShapeDtypeStruct((DB, 1, ATT_W), BF16),
        grid_spec=pltpu.PrefetchScalarGridSpec(
            num_scalar_prefetch=1, grid=(DB,), in_specs=in_specs,
            out_specs=pl.BlockSpec((1, 1, ATT_W), lambda b, pt: (b, 0, 0))),
        compiler_params=_params("arbitrary"), name="dsa_sample",
    )(page_table, aq.reshape(DB, 1, ATT_W), iq8, iw8, ak.reshape(DB, 1, ATT_W), av.reshape(DB, 1, ATT_W),
      ik.reshape(DB, 1, IDX_DIM), *([ik_t] * n_pages), *([ck_t] * n_pages), *([cv_t] * n_pages))
    return out.reshape(DB, ATT_W)


def _retention_tables(C):
    lg = jnp.log(1.0 - 2.0 ** (-5.0 - jnp.arange(RET_HEADS, dtype=F32)))
    i = jnp.arange(C, dtype=F32)
    diff = i[:, None] - i[None, :]
    causal = diff >= 0
    decay = jnp.where(causal[None], jnp.exp(jnp.where(causal, diff, 0.0)[None] * lg[:, None, None]), 0.0)
    row = jnp.exp((i[None, :] + 1.0) * lg[:, None])
    wk = jnp.exp((C - 1.0 - i)[None, :] * lg[:, None])
    full = jnp.exp(C * lg)
    return (decay, jnp.broadcast_to(row[:, :, None], (RET_HEADS, C, RET_DV)),
            jnp.broadcast_to(wk[:, :, None], (RET_HEADS, C, RET_DK)),
            jnp.broadcast_to(full[:, None, None], (RET_HEADS, 1, RET_DV)))


def _ret_prompt_kernel(q_ref, k_ref, v_ref, dm_ref, rd_ref, kd_ref, gc_ref, o_ref, s_ref, st_scr):
    c = pl.program_id(1)

    @pl.when(c == 0)
    def _():
        st_scr[...] = jnp.zeros(st_scr.shape, F32)

    for h in range(RET_HEADS):
        ks = slice(h * RET_DK, (h + 1) * RET_DK)
        vs = slice(h * RET_DV, (h + 1) * RET_DV)
        qh = q_ref[0, :, ks]
        kh = k_ref[0, :, ks]
        vh = v_ref[0, :, vs]
        st = st_scr[h]
        sc = _dot_nt(qh, kh.astype(BF16)) * dm_ref[h]
        o = _dot(sc.astype(BF16), vh) + _dot(qh, st.astype(BF16)) * rd_ref[h]
        kw = (kh * kd_ref[h]).astype(BF16)
        st_scr[h] = gc_ref[h] * st + _dot_tn(kw, vh)
        o_ref[0, :, vs] = _norm_rows(o)

    @pl.when(c == pl.num_programs(1) - 1)
    def _():
        s_ref[0] = st_scr[...]


def prompt_retention(rq, rk, rv):
    B, L, _ = rq.shape
    C = RET_CHUNK
    dm, rd, kd, gc = _retention_tables(C)

    def full(a):
        return pl.BlockSpec(a.shape, lambda b, c: (0,) * a.ndim)

    return pl.pallas_call(
        _ret_prompt_kernel,
        out_shape=[jax.ShapeDtypeStruct((B, L, RET_V), F32),
                   jax.ShapeDtypeStruct((B, RET_HEADS, RET_DK, RET_DV), F32)],
        grid=(B, L // C),
        in_specs=[pl.BlockSpec((1, C, RET_QK), lambda b, c: (b, c, 0)),
                  pl.BlockSpec((1, C, RET_QK), lambda b, c: (b, c, 0)),
                  pl.BlockSpec((1, C, RET_V), lambda b, c: (b, c, 0)),
                  full(dm), full(rd), full(kd), full(gc)],
        out_specs=[pl.BlockSpec((1, C, RET_V), lambda b, c: (b, c, 0)),
                   pl.BlockSpec((1, RET_HEADS, RET_DK, RET_DV), lambda b, c: (b, 0, 0, 0))],
        scratch_shapes=[pltpu.VMEM((RET_HEADS, RET_DK, RET_DV), F32)],
        compiler_params=_params("arbitrary", "arbitrary"), name="retention_prompt",
    )(rq, rk, rv, dm, rd, kd, gc)


def _ret_sample_kernel(q_ref, k_ref, v_ref, st_ref, rd_ref, gc_ref, o_ref, s_ref):
    G = q_ref.shape[0]
    for g in range(G):
        outs = []
        for h in range(RET_HEADS):
            qc = q_ref[g, h]
            kc = k_ref[g, h]
            vr = v_ref[g, h]
            st = st_ref[0, g, h]
            qk = jnp.sum(qc * kc, axis=0, keepdims=True)
            o = qk * vr + jnp.sum(qc * st, axis=0, keepdims=True) * rd_ref[h]
            s_ref[g, h] = gc_ref[h] * st + kc * vr
            outs.append(_norm_rows(o))
        o_ref[pl.ds(g, 1), :] = jnp.concatenate(outs, axis=1)


def sample_retention(layer, rq, rk, rv, state_ret):
    DB = rq.shape[0]
    G = 8
    _, rd, _, gc = _retention_tables(1)
    qT = rq.reshape(DB, RET_HEADS, RET_DK, 1)
    kT = rk.reshape(DB, RET_HEADS, RET_DK, 1)
    v4 = rv.reshape(DB, RET_HEADS, 1, RET_DV)
    col = pl.BlockSpec((G, RET_HEADS, RET_DK, 1), lambda i: (i, 0, 0, 0))
    return pl.pallas_call(
        _ret_sample_kernel,
        out_shape=[jax.ShapeDtypeStruct((DB, RET_V), F32),
                   jax.ShapeDtypeStruct((DB, RET_HEADS, RET_DK, RET_DV), F32)],
        grid=(DB // G,),
        in_specs=[col, col,
                  pl.BlockSpec((G, RET_HEADS, 1, RET_DV), lambda i: (i, 0, 0, 0)),
                  pl.BlockSpec((1, G, RET_HEADS, RET_DK, RET_DV), lambda i: (layer, i, 0, 0, 0)),
                  pl.BlockSpec(rd.shape, lambda i: (0, 0, 0)),
                  pl.BlockSpec(gc.shape, lambda i: (0, 0, 0))],
        out_specs=[pl.BlockSpec((G, RET_V), lambda i: (i, 0)),
                   pl.BlockSpec((G, RET_HEADS, RET_DK, RET_DV), lambda i: (i, 0, 0, 0))],
        compiler_params=_params("arbitrary"), name="retention_sample",
    )(qT, kT, v4, state_ret, rd, gc)


_HALO = 32


def _conv_post(y, g_ref, b_ref):
    return _silu(_norm_rows(y) * g_ref[...] + b_ref[...])


def _conv_prompt_kernel(z_ref, dw_ref, db_ref, g_ref, b_ref, o_ref, zp_scr, *, tl, rs):
    @pl.when(pl.program_id(1) == 0)
    def _():
        zp_scr[0:_HALO, :] = jnp.zeros((_HALO, CONV_CH), F32)

    zp_scr[_HALO:_HALO + tl, :] = z_ref[0]
    first = _HALO - (CONV_W - 1)
    for r in range(tl // rs):
        acc = jnp.broadcast_to(db_ref[...], (rs, CONV_CH))
        for k in range(CONV_W):
            lo = first + k + r * rs
            acc = acc + zp_scr[lo:lo + rs, :] * dw_ref[k:k + 1, :]
        o_ref[0, r * rs:(r + 1) * rs, :] = _conv_post(acc, g_ref, b_ref).astype(o_ref.dtype)
    zp_scr[0:_HALO, :] = zp_scr[tl:tl + _HALO, :]


def prompt_conv(z, dw, db, ln_g, ln_b):
    B, L, _ = z.shape
    tl = min(512, L)
    rs = min(128, tl)
    vec = pl.BlockSpec((1, CONV_CH), lambda b, i: (0, 0))
    return pl.pallas_call(
        functools.partial(_conv_prompt_kernel, tl=tl, rs=rs),
        out_shape=jax.ShapeDtypeStruct((B, L, CONV_CH), BF16),
        grid=(B, L // tl),
        in_specs=[pl.BlockSpec((1, tl, CONV_CH), lambda b, i: (b, i, 0)),
                  pl.BlockSpec((CONV_W, CONV_CH), lambda b, i: (0, 0)), vec, vec, vec],
        out_specs=pl.BlockSpec((1, tl, CONV_CH), lambda b, i: (b, i, 0)),
        scratch_shapes=[pltpu.VMEM((_HALO + tl, CONV_CH), F32)],
        compiler_params=_params("arbitrary", "arbitrary"), name="conv_prompt",
    )(z, dw, db.reshape(1, -1), ln_g.reshape(1, -1), ln_b.reshape(1, -1))


def _conv_sample_kernel(z_ref, buf_ref, dw_ref, db_ref, g_ref, b_ref, o_ref, nb_ref):
    G = z_ref.shape[0]
    w_past = dw_ref[0:CONV_W - 1, :]
    w_last = dw_ref[CONV_W - 1:CONV_W, :]
    rows = []
    for g in range(G):
        buf = buf_ref[0, g]
        zr = z_ref[pl.ds(g, 1), :]
        rows.append(jnp.sum(buf * w_past, axis=0, keepdims=True) + zr * w_last + db_ref[...])
        nb_ref[g, 0:CONV_W - 2, :] = buf_ref[0, g, 1:CONV_W - 1, :]
        nb_ref[g, CONV_W - 2:CONV_W - 1, :] = zr
    o_ref[...] = _conv_post(jnp.concatenate(rows, axis=0), g_ref, b_ref).astype(o_ref.dtype)


def sample_conv(layer, z, state_conv, dw, db, ln_g, ln_b):
    DB = z.shape[0]
    G = 8
    vec = pl.BlockSpec((1, CONV_CH), lambda i: (0, 0))
    return pl.pallas_call(
        _conv_sample_kernel,
        out_shape=[jax.ShapeDtypeStruct((DB, CONV_CH), BF16),
                   jax.ShapeDtypeStruct((DB, CONV_W - 1, CONV_CH), F32)],
        grid=(DB // G,),
        in_specs=[pl.BlockSpec((G, CONV_CH), lambda i: (i, 0)),
                  pl.BlockSpec((1, G, CONV_W - 1, CONV_CH), lambda i: (layer, i, 0, 0)),
                  pl.BlockSpec((CONV_W, CONV_CH), lambda i: (0, 0)), vec, vec, vec],
        out_specs=[pl.BlockSpec((G, CONV_CH), lambda i: (i, 0)),
                   pl.BlockSpec((G, CONV_W - 1, CONV_CH), lambda i: (i, 0, 0))],
        compiler_params=_params("arbitrary"), name="conv_sample",
    )(z, state_conv, dw, db.reshape(1, -1), ln_g.reshape(1, -1), ln_b.reshape(1, -1))


def _merge_kernel(x_ref, oret_ref, rg_ref, act_ref, att_ref, gs_ref, g1_ref, sc2_ref, sh2_ref,
                  wr_ref, wc_ref, bc_ref, wa_ref, wo_ref, lg_ref, lb_ref, *rest):
    x1_ref, h_ref = rest[-2], rest[-1]
    D = D_MODEL
    y_ret = _dot((_silu(rg_ref[0]) * oret_ref[0]).astype(BF16), wr_ref[...])
    y_conv = _dot(act_ref[0], wc_ref[...]) + bc_ref[...]
    y_att = _dot(att_ref[0], wa_ref[...])
    m = gs_ref[0, :, 0:D] * y_ret + gs_ref[0, :, D:2 * D] * y_conv + gs_ref[0, :, 2 * D:3 * D] * y_att
    mix = _dot(m.astype(BF16), wo_ref[...])
    x1 = _norm_rows(ALPHA * x_ref[0] + g1_ref[0] * mix) * lg_ref[...] + lb_ref[...]
    x1_ref[0] = x1
    h_ref[...] = x1 * (1.0 + sc2_ref[0]) + sh2_ref[0]


def merge_and_norm(x, oret, rg, act, att, gsig, g1, sc2, sh2, w_ret, w_conv, b_conv, w_att, w_o, ln_g, ln_b,
                   h_all, n_all, row0, tm):
    B, L, D = x.shape
    nl = L // tm
    blk0 = row0 // tm

    def rows(width):
        return pl.BlockSpec((1, tm, width), lambda b, i: (b, i, 0))

    def full(a):
        return pl.BlockSpec(a.shape, lambda b, i: (0,) * a.ndim)

    vec = pl.BlockSpec((1, D), lambda b, i: (0, 0))
    h_spec = pl.BlockSpec((tm, D), lambda b, i: (blk0 + b * nl + i, 0))
    ms = _mod_spec(g1, tm)
    args = [x, oret, rg, act, att, gsig, g1, sc2, sh2, w_ret, w_conv, b_conv.reshape(1, D), w_att, w_o,
            ln_g.reshape(1, D), ln_b.reshape(1, D)]
    in_specs = [rows(D), rows(RET_V), rows(RET_V), rows(CONV_CH), rows(ATT_W), rows(N_BRANCH * D), ms, ms, ms,
                full(w_ret), full(w_conv), vec, full(w_att), full(w_o), vec, vec]
    aliases = {}
    if h_all is not None:
        args.append(h_all)
        in_specs.append(pl.BlockSpec(memory_space=pl.ANY))
        aliases = {len(args) - 1: 1}
    return pl.pallas_call(
        _merge_kernel,
        out_shape=[jax.ShapeDtypeStruct((B, L, D), F32), jax.ShapeDtypeStruct((n_all, D), F32)],
        grid=(B, nl), in_specs=in_specs,
        out_specs=[rows(D), h_spec],
        input_output_aliases=aliases,
        compiler_params=_params("arbitrary", "arbitrary"), name="merge_norm",
    )(*args)


def _router_kernel(h_ref, whi_ref, wlo_ref, b_ref, e_ref, g_ref):
    h = h_ref[...]
    h_hi = h.astype(BF16)
    h_lo = (h - h_hi.astype(F32)).astype(BF16)
    logits = (_dot(h_hi, whi_ref[...]) + (_dot(h_hi, wlo_ref[...]) + _dot(h_lo, whi_ref[...]))) + b_ref[...]
    lane = lax.broadcasted_iota(I32, logits.shape, 1).astype(F32)
    vals = logits
    e_out = jnp.zeros(logits.shape, F32)
    top = []
    for k in range(MOE_TOP_K):
        m = jnp.max(vals, axis=1, keepdims=True)
        idx = jnp.min(jnp.where(vals == m, lane, float(LANES)), axis=1, keepdims=True)
        e_out = jnp.where(lane == float(k), idx, e_out)
        vals = jnp.where(lane == idx, -jnp.inf, vals)
        top.append(m)
    ex = [jnp.exp(t - top[0]) for t in top]
    den = ex[0] + ex[1] + ex[2] + ex[3]
    g_out = jnp.zeros(logits.shape, F32)
    for k in range(MOE_TOP_K):
        g_out = jnp.where(lane == float(k), ex[k] / den, g_out)
    e_ref[...] = e_out.astype(I32)
    g_ref[...] = g_out


def router_top4(h_all, router_w, router_b, tm):
    N, D = h_all.shape
    wp = jnp.zeros((D, LANES), F32).at[:, :N_EXPERTS].set(router_w)
    w_hi = wp.astype(BF16)
    w_lo = (wp - w_hi.astype(F32)).astype(BF16)
    bp = jnp.full((1, LANES), NEG_BIG, F32).at[0, :N_EXPERTS].set(router_b)
    e, g = pl.pallas_call(
        _router_kernel,
        out_shape=[jax.ShapeDtypeStruct((N, LANES), I32), jax.ShapeDtypeStruct((N, LANES), F32)],
        grid=(N // tm,),
        in_specs=[pl.BlockSpec((tm, D), lambda i: (i, 0)),
                  pl.BlockSpec((D, LANES), lambda i: (0, 0)),
                  pl.BlockSpec((D, LANES), lambda i: (0, 0)),
                  pl.BlockSpec((1, LANES), lambda i: (0, 0))],
        out_specs=[pl.BlockSpec((tm, LANES), lambda i: (i, 0)), pl.BlockSpec((tm, LANES), lambda i: (i, 0))],
        compiler_params=_params("arbitrary"), name="router_top4",
    )(h_all, w_hi, w_lo, bp)
    return e[:, :MOE_TOP_K], g[:, :MOE_TOP_K]


def _moe_kernel(blk_e_ref, nreal_ref, idx_hbm, h_hbm, g_ref, wgu_ref, bgu_ref, wd_ref, bd_ref, y_hbm,
                idx_smem, xbuf, obuf, hid_scr, wgu_bf, wd_bf, sem_idx, sem_g, sem_s, *, bm, nc):
    i = pl.program_id(0)
    n = pl.num_programs(0)

    def idx_copy(c):
        slot = c % 3
        return pltpu.make_async_copy(idx_hbm.at[c], idx_smem.at[pl.ds(slot * 2 * bm, 2 * bm)], sem_idx.at[slot])

    def issue_gather(c):
        base = (c % 3) * 2 * bm
        buf = c % 2

        @pl.when(nreal_ref[c] > 0)
        def _():
            dst = xbuf.at[buf]
            for r in range(bm):
                tok = idx_smem[base + r]
                pltpu.make_async_copy(h_hbm.at[pl.ds(tok, 1)], dst.at[pl.ds(r, 1)], sem_g.at[buf]).start()

    def wait_gather(c):
        buf = c % 2

        @pl.when(nreal_ref[c] > 0)
        def _():
            pltpu.make_async_copy(h_hbm.at[pl.ds(0, bm)], xbuf.at[buf], sem_g.at[buf]).wait()

    def wait_scatter(c):
        @pl.when(nreal_ref[c] > 0)
        def _():
            pltpu.make_async_copy(obuf, y_hbm.at[pl.ds(0, bm)], sem_s).wait()

    @pl.when(i == 0)
    def _():
        idx_copy(0).start()
        idx_copy(0).wait()
        issue_gather(0)

        @pl.when(n > 1)
        def _():
            idx_copy(1).start()

    @pl.when(i + 1 < n)
    def _():
        idx_copy(i + 1).wait()
        issue_gather(i + 1)

        @pl.when(i + 2 < n)
        def _():
            idx_copy(i + 2).start()

    nr = nreal_ref[i]
    buf = i % 2
    wait_gather(i)

    @pl.when((i == 0) | (blk_e_ref[i] != blk_e_ref[jnp.maximum(i - 1, 0)]))
    def _():
        wgu_bf[...] = wgu_ref[0].astype(BF16)
        wd_bf[...] = wd_ref[0].astype(BF16)

    @pl.when(i > 0)
    def _():
        wait_scatter(jnp.maximum(i - 1, 0))

    @pl.when(nr > 0)
    def _():
        x = xbuf[buf].astype(BF16)
        for j in range(D_EXPERT // nc):
            lo = j * nc
            a = _dot(x, wgu_bf[:, lo:lo + nc]) + bgu_ref[0, :, lo:lo + nc]
            lin = _dot(x, wgu_bf[:, D_EXPERT + lo:D_EXPERT + lo + nc]) + bgu_ref[0, :, D_EXPERT + lo:D_EXPERT + lo + nc]
            a = jnp.minimum(a, SWIGLU_LIMIT)
            lin = jnp.clip(lin, -SWIGLU_LIMIT, SWIGLU_LIMIT)
            hid_scr[:, lo:lo + nc] = (a * jax.nn.sigmoid(SWIGLU_ALPHA * a) * (lin + 1.0)).astype(BF16)
        obuf[...] = (_dot(hid_scr[...], wd_bf[...]) + bd_ref[0]) * g_ref[0]

        base = (i % 3) * 2 * bm + bm
        for r in range(bm):
            dst = idx_smem[base + r]
            pltpu.make_async_copy(obuf.at[pl.ds(r, 1)], y_hbm.at[pl.ds(dst, 1)], sem_s).start()

    @pl.when(i == n - 1)
    def _():
        wait_scatter(i)


def moe_dispatch(top_e, gate, bm):
    n_tok = top_e.shape[0]
    n_assign = n_tok * MOE_TOP_K
    flat_e = top_e.reshape(-1)
    order = jnp.argsort(flat_e).astype(I32)
    se = flat_e[order]
    stok = order // MOE_TOP_K
    sk = order % MOE_TOP_K
    sg = gate.reshape(-1)[order]
    counts = jnp.sum((flat_e[:, None] == jnp.arange(N_EXPERTS, dtype=I32)[None, :]).astype(I32), axis=0)
    starts = jnp.cumsum(counts) - counts
    rank = jnp.arange(n_assign, dtype=I32) - starts[se]
    pcounts = (counts + bm - 1) // bm * bm
    pends = jnp.cumsum(pcounts)
    pstarts = pends - pcounts
    ppos = pstarts[se] + rank
    n_chunks = -(-n_assign // bm) + N_EXPERTS
    n_slots = n_chunks * bm
    spare = n_assign + jnp.arange(n_slots, dtype=I32) % bm
    init = jnp.stack([spare, jnp.zeros((n_slots,), I32)], axis=1)
    vals = jnp.stack([sk * n_tok + stok, lax.bitcast_convert_type(sg, I32)], axis=1)
    packed = init.at[ppos].set(vals)
    buf_dst = packed[:, 0]
    buf_tok = jnp.where(buf_dst < n_assign, buf_dst % n_tok, 0)
    buf_g = lax.bitcast_convert_type(packed[:, 1], F32)
    chunk0 = jnp.arange(n_chunks, dtype=I32) * bm
    blk_e = jnp.minimum(jnp.sum((chunk0[:, None] >= pends[None, :]).astype(I32), axis=1), N_EXPERTS - 1)
    nreal = jnp.clip(pstarts[blk_e] + counts[blk_e] - chunk0, 0, bm).astype(I32)
    idx = jnp.concatenate([buf_tok.reshape(n_chunks, bm), buf_dst.reshape(n_chunks, bm)], axis=1)
    return blk_e, nreal, idx, buf_g.reshape(n_chunks, bm, 1)


def moe_experts(h_all, top_e, gate, layer, exp_w_gu, exp_b_gu, exp_w_down, exp_b_down):
    blk_e, nreal, idx, gcol = moe_dispatch(top_e, gate, MOE_BM)
    return moe_grouped_matmul(blk_e, nreal, idx, gcol, h_all, layer, exp_w_gu, exp_b_gu, exp_w_down, exp_b_down)


def moe_grouped_matmul(blk_e, nreal, idx, gcol, h_all, layer, exp_w_gu, exp_b_gu, exp_w_down, exp_b_down):
    N, D = h_all.shape
    bm = MOE_BM
    nc = 256
    n_chunks = idx.shape[0]
    b_gu = exp_b_gu.reshape(DEPTH, N_EXPERTS, 1, 2 * D_EXPERT)
    b_dn = exp_b_down.reshape(DEPTH, N_EXPERTS, 1, D)
    grid_spec = pltpu.PrefetchScalarGridSpec(
        num_scalar_prefetch=2, grid=(n_chunks,),
        in_specs=[pl.BlockSpec(memory_space=pl.ANY),
                  pl.BlockSpec(memory_space=pl.ANY),
                  pl.BlockSpec((1, bm, 1), lambda i, be, nr: (i, 0, 0)),
                  pl.BlockSpec((1, 1, D, 2 * D_EXPERT), lambda i, be, nr: (layer, be[i], 0, 0)),
                  pl.BlockSpec((1, 1, 1, 2 * D_EXPERT), lambda i, be, nr: (layer, be[i], 0, 0)),
                  pl.BlockSpec((1, 1, D_EXPERT, D), lambda i, be, nr: (layer, be[i], 0, 0)),
                  pl.BlockSpec((1, 1, 1, D), lambda i, be, nr: (layer, be[i], 0, 0))],
        out_specs=pl.BlockSpec(memory_space=pl.ANY),
        scratch_shapes=[pltpu.SMEM((3 * 2 * bm,), I32),
                        pltpu.VMEM((2, bm, D), F32),
                        pltpu.VMEM((bm, D), F32),
                        pltpu.VMEM((bm, D_EXPERT), BF16),
                        pltpu.VMEM((D, 2 * D_EXPERT), BF16),
                        pltpu.VMEM((D_EXPERT, D), BF16),
                        pltpu.SemaphoreType.DMA((3,)),
                        pltpu.SemaphoreType.DMA((2,)),
                        pltpu.SemaphoreType.DMA(())])
    y = pl.pallas_call(
        functools.partial(_moe_squeeze_kernel, bm=bm, nc=nc),
        out_shape=jax.ShapeDtypeStruct((MOE_TOP_K * N + bm, D), F32),
        grid_spec=grid_spec,
        compiler_params=_params("arbitrary"), name="moe_experts",
    )(blk_e, nreal, idx, h_all, gcol, exp_w_gu, b_gu, exp_w_down, b_dn)
    return y


def _moe_squeeze_kernel(blk_e_ref, nreal_ref, idx_hbm, h_hbm, g_ref, wgu_ref, bgu_ref, wd_ref, bd_ref, y_hbm,
                        *scratch, bm, nc):
    _moe_kernel(blk_e_ref, nreal_ref, idx_hbm, h_hbm, g_ref, wgu_ref.at[0], bgu_ref.at[0], wd_ref.at[0],
                bd_ref.at[0], y_hbm, *scratch, bm=bm, nc=nc)


def _combine_kernel(x_ref, y0_ref, y1_ref, y2_ref, y3_ref, g2_ref, lg_ref, lb_ref, o_ref):
    ff = (y0_ref[...] + y1_ref[...]) + (y2_ref[...] + y3_ref[...])
    o_ref[0] = _norm_rows(ALPHA * x_ref[0] + g2_ref[0] * ff) * lg_ref[...] + lb_ref[...]


def combine_and_norm(x1, y, n_all, g2, ln_g, ln_b, row0, tm):
    B, L, D = x1.shape
    nl = L // tm
    blk0 = row0 // tm
    per_k = n_all // tm
    vec = pl.BlockSpec((1, D), lambda b, i: (0, 0))

    def y_spec(k):
        return pl.BlockSpec((tm, D), lambda b, i, k=k: (k * per_k + blk0 + b * nl + i, 0))

    return pl.pallas_call(
        _combine_kernel,
        out_shape=jax.ShapeDtypeStruct((B, L, D), F32),
        grid=(B, nl),
        in_specs=[pl.BlockSpec((1, tm, D), lambda b, i: (b, i, 0))] + [y_spec(k) for k in range(MOE_TOP_K)]
        + [_mod_spec(g2, tm), vec, vec],
        out_specs=pl.BlockSpec((1, tm, D), lambda b, i: (b, i, 0)),
        compiler_params=_params("arbitrary", "arbitrary"), name="combine_norm",
    )(x1, y, y, y, y, g2, ln_g.reshape(1, D), ln_b.reshape(1, D))


def _rope_tables(pos):
    half = HEAD_DIM // 2
    inv = 1.0 / (ROPE_THETA ** (jnp.arange(half, dtype=F32) / half))
    ang = pos.astype(F32)[:, None] * inv[None, :]
    cos, sin = jnp.cos(ang), jnp.sin(ang)
    reps = LANES // HEAD_DIM
    return jnp.tile(jnp.concatenate([cos, cos], axis=1), (1, reps)), jnp.tile(jnp.concatenate([-sin, sin], axis=1), (1, reps))


def _split_w_in(w):
    o = IN_OFFSETS
    aq, ak, av, iq, ik, iw, rq, rk, rv, rg, cglu, gl = [w[:, o[i]:o[i + 1]] for i in range(len(IN_SPLITS))]
    D = w.shape[0]
    pad_ik = jnp.zeros((D, _R_END - _R_IK - IDX_DIM), F32)
    pad_iw = jnp.zeros((D, _P_END - _P_IW - IDX_HEADS), F32)
    w_rope = jnp.concatenate([aq * (HEAD_DIM ** -0.5), ak, iq, rq, rk * (RET_DK ** -0.5), ik, pad_ik], axis=1)
    w_plain = jnp.concatenate([av, rv, rg, cglu, iw * ((IDX_DIM ** -0.5) * (IDX_HEADS ** -0.5)), pad_iw], axis=1)
    return w_rope.astype(BF16), w_plain.astype(BF16), gl.astype(BF16)


def kernel(x_prompt, x_sample, c_prompt, c_sample, cache_k, cache_v, cache_idx_k, state_ret, state_conv, page_table, ada_w, ada_b, w_in, w_ret_out, conv_dw, conv_db, conv_ln_g, conv_ln_b, w_conv_out, b_conv_out, w_att_out, w_o, ln1_g, ln1_b, router_w, router_b, exp_w_gu, exp_b_gu, exp_w_down, exp_b_down, ln2_g, ln2_b):
    B, L, D = x_prompt.shape
    DB = x_sample.shape[0]
    n_p = B * L
    n_all = n_p + DB
    past = page_table.shape[1] * PAGE_SIZE

    r_pad = -(B + DB) % 8
    c_all = jnp.concatenate([c_prompt, c_sample, jnp.zeros((r_pad, D), F32)], axis=0)
    mod = ada_modulation(c_all, ada_w, ada_b)

    cos_p, sin_p = _rope_tables(jnp.arange(L, dtype=I32))
    cos_s, sin_s = _rope_tables(jnp.full((DB,), past, I32))

    xp = x_prompt
    xs = x_sample.reshape(1, DB, D)
    new_p = ([], [], [], [], [])
    new_s = ([], [], [], [], [])
    tm_p = min(PROJ_TM, L)
    tm_m = min(256, L)
    tm_r = math.gcd(n_all, 384)
    tm_c = math.gcd(math.gcd(L, DB), 128)

    for l in range(DEPTH):
        mp = [mod[l, :B, i * D:(i + 1) * D].reshape(B, 1, D) for i in range(6)]
        ms = [mod[l, B:B + DB, i * D:(i + 1) * D].reshape(1, DB, D) for i in range(6)]
        w_rope, w_plain, w_gate = _split_w_in(w_in[l])
        wr, wc, wa, wo = (w_ret_out[l].astype(BF16), w_conv_out[l].astype(BF16), w_att_out[l].astype(BF16),
                          w_o[l].astype(BF16))

        pp = input_projection(xp, mp[1], mp[0], cos_p, sin_p, w_rope, w_plain, w_gate, tm_p)
        ps = input_projection(xs, ms[1], ms[0], cos_s, sin_s, w_rope, w_plain, w_gate, DB)

        att_p = prompt_sparse_attention(pp['aq'], pp['iq'], pp['iw'], pp['akt'], pp['avh'], pp['ikt'])
        oret_p, sret_p = prompt_retention(pp['rq'], pp['rk'], pp['rv'])
        act_p = prompt_conv(pp['z'], conv_dw[l], conv_db[l], conv_ln_g[l], conv_ln_b[l])

        aq_s = jnp.transpose(ps['aq'][0], (1, 0, 2)).reshape(DB, ATT_W).astype(F32)
        iq_s = jnp.transpose(ps['iq'][0], (1, 0, 2)).reshape(DB, IDX_HEADS * IDX_DIM)
        att_s = sample_sparse_attention(l, aq_s, iq_s, ps['iw'][0], ps['ak'][0],
                                        ps['av'][0], ps['ik'][0], cache_k, cache_v, cache_idx_k, page_table)
        oret_s, sret_s = sample_retention(l, ps['rq'][0].astype(F32), ps['rk'][0], ps['rv'][0].astype(F32), state_ret)
        act_s, conv_s = sample_conv(l, ps['z'][0], state_conv, conv_dw[l], conv_db[l], conv_ln_g[l], conv_ln_b[l])

        x1p, h_all = merge_and_norm(xp, oret_p, pp['rg'], act_p, att_p, pp['gsig'], mp[2], mp[4], mp[3],
                                    wr, wc, b_conv_out[l], wa, wo, ln1_g[l], ln1_b[l], None, n_all, 0, tm_m)
        x1s, h_all = merge_and_norm(xs, oret_s.reshape(1, DB, -1), ps['rg'], act_s.reshape(1, DB, -1),
                                    att_s.reshape(1, DB, -1), ps['gsig'], ms[2], ms[4], ms[3],
                                    wr, wc, b_conv_out[l], wa, wo, ln1_g[l], ln1_b[l], h_all, n_all, n_p, DB)

        top_e, gate = router_top4(h_all, router_w[l], router_b[l], tm_r)
        y4 = moe_experts(h_all, top_e, gate, l, exp_w_gu, exp_b_gu, exp_w_down, exp_b_down)
        xp = combine_and_norm(x1p, y4, n_all, mp[5], ln2_g[l], ln2_b[l], 0, tm_c)
        xs = combine_and_norm(x1s, y4, n_all, ms[5], ln2_g[l], ln2_b[l], n_p, tm_c)

        for lst, a in zip(new_p, (pp['ak'].reshape(B, L, N_HEADS, HEAD_DIM), pp['av'].reshape(B, L, N_HEADS, HEAD_DIM),
                                  pp['ik'], sret_p, pp['z'][:, L - (CONV_W - 1):, :])):
            lst.append(a)
        for lst, a in zip(new_s, (ps['ak'].reshape(DB, 1, N_HEADS, HEAD_DIM), ps['av'].reshape(DB, 1, N_HEADS, HEAD_DIM),
                                  ps['ik'].reshape(DB, 1, IDX_DIM), sret_s, conv_s)):
            lst.append(a)

    nk_p, nv_p, nik_p, nr_p, nc_p = [jnp.stack(a) for a in new_p]
    nk_s, nv_s, nik_s, nr_s, nc_s = [jnp.stack(a) for a in new_s]
    return (xp, xs.reshape(DB, 1, D), nk_p, nv_p, nik_p, nr_p, nc_p, nk_s, nv_s, nik_s, nr_s, nc_s)
```

```python
import functools
import math

import numpy as np
import jax
import jax.numpy as jnp
from jax import lax
from jax.experimental import pallas as pl
from jax.experimental.pallas import tpu as pltpu

F32 = jnp.float32
BF16 = jnp.bfloat16
I32 = jnp.int32

D_MODEL = 1024
DEPTH = 4
PAGE_SIZE = 128
N_HEADS = 8
HEAD_DIM = 64
ATT_W = N_HEADS * HEAD_DIM
IDX_HEADS = 4
IDX_DIM = 64
TOPK_MAX = 256
Q_BLOCK = 128
ROPE_THETA = 10000.0
RET_HEADS = 4
RET_DK = 64
RET_DV = 128
RET_QK = RET_HEADS * RET_DK
RET_V = RET_HEADS * RET_DV
RET_CHUNK = 128
CONV_CH = 512
CONV_W = 31
N_EXPERTS = 32
MOE_TOP_K = 4
D_EXPERT = 1024
SWIGLU_LIMIT = 7.0
SWIGLU_ALPHA = 1.702
N_BRANCH = 3
ALPHA = (2 * DEPTH) ** 0.25
LN_EPS = 1e-5
IN_SPLITS = (ATT_W, ATT_W, ATT_W, IDX_HEADS * IDX_DIM, IDX_DIM, IDX_HEADS, RET_QK, RET_QK, RET_V, RET_V,
             2 * CONV_CH, N_BRANCH * D_MODEL)
IN_OFFSETS = tuple(int(o) for o in np.cumsum((0,) + IN_SPLITS))

LANES = 128
INT_MIN = -2 ** 31
NEG_BIG = -1e30
MOE_BM = 256
PROJ_TM = 512
HEAD_GROUP = 4
VMEM_LIMIT = 56 * 1024 * 1024


def _dot(a, b):
    return jnp.dot(a, b, preferred_element_type=F32)


def _dot_nt(a, b):
    return lax.dot_general(a, b, (((1,), (1,)), ((), ())), preferred_element_type=F32)


def _dot_tn(a, b):
    return lax.dot_general(a, b, (((0,), (0,)), ((), ())), preferred_element_type=F32)


def _silu(x):
    return x * jax.nn.sigmoid(x)


def _norm_rows(x):
    mu = jnp.mean(x, axis=-1, keepdims=True)
    xc = x - mu
    var = jnp.mean(xc * xc, axis=-1, keepdims=True)
    return xc * lax.rsqrt(var + LN_EPS)


def _params(*sem):
    return pltpu.CompilerParams(dimension_semantics=sem, vmem_limit_bytes=VMEM_LIMIT)


def _ada_kernel(c_ref, w_ref, b_ref, o_ref):
    c = c_ref[...]
    o_ref[0] = _dot(_silu(c).astype(BF16), w_ref[0].astype(BF16)) + b_ref[0]


def ada_modulation(c_all, ada_w, ada_b):
    R, D = c_all.shape
    W = ada_w.shape[-1]
    tn = 1536
    return pl.pallas_call(
        _ada_kernel,
        out_shape=jax.ShapeDtypeStruct((DEPTH, R, W), F32),
        grid=(DEPTH, W // tn),
        in_specs=[pl.BlockSpec((R, D), lambda l, j: (0, 0)),
                  pl.BlockSpec((1, D, tn), lambda l, j: (l, 0, j)),
                  pl.BlockSpec((1, 1, tn), lambda l, j: (l, 0, j))],
        out_specs=pl.BlockSpec((1, R, tn), lambda l, j: (l, 0, j)),
        compiler_params=_params("arbitrary", "arbitrary"),
        name="ada_modulation",
    )(c_all, ada_w, ada_b.reshape(DEPTH, 1, W))


def _rope_chunks(y, cos, sin, first_half):
    parts = []
    for c in range(y.shape[1] // LANES):
        yc = y[:, c * LANES:(c + 1) * LANES]
        rot = jnp.where(first_half, pltpu.roll(yc, LANES - HEAD_DIM // 2, 1), pltpu.roll(yc, HEAD_DIM // 2, 1))
        parts.append(yc * cos + rot * sin)
    return parts[0] if len(parts) == 1 else jnp.concatenate(parts, axis=1)


_R_AQ, _R_AK, _R_IQ, _R_RQ, _R_RK, _R_IK, _R_END = 0, 512, 1024, 1280, 1536, 1792, 1920


def _proj_rope_kernel(x_ref, sc_ref, sh_ref, cos_ref, sin_ref, w_ref,
                      aq_ref, ak_ref, akt_ref, iq_ref, rq_ref, rk_ref, ik_ref, ikt_ref):
    u = (x_ref[0] * (1.0 + sc_ref[0]) + sh_ref[0]).astype(BF16)
    cos = cos_ref[...]
    sin = sin_ref[...]
    lane = lax.broadcasted_iota(I32, cos.shape, 1)
    first_half = (lane % HEAD_DIM) < (HEAD_DIM // 2)

    def proj(lo, hi):
        return _rope_chunks(_dot(u, w_ref[:, lo:hi]), cos, sin, first_half)

    aq = proj(_R_AQ, _R_AK).astype(BF16)
    for h in range(N_HEADS):
        aq_ref[0, h] = aq[:, h * HEAD_DIM:(h + 1) * HEAD_DIM]
    ak = proj(_R_AK, _R_IQ)
    ak_ref[0] = ak
    akt_ref[0, 0] = ak.T.astype(BF16)
    iq = proj(_R_IQ, _R_RQ).astype(BF16)
    for h in range(IDX_HEADS):
        iq_ref[0, h] = iq[:, h * IDX_DIM:(h + 1) * IDX_DIM]
    rq_ref[0] = proj(_R_RQ, _R_RK).astype(BF16)
    rk_ref[0] = proj(_R_RK, _R_IK)
    ik2 = proj(_R_IK, _R_END)
    ik_ref[0] = ik2[:, :IDX_DIM]
    ikt_ref[0, 0] = ik2.T[:IDX_DIM, :].astype(BF16)


_P_AV, _P_RV, _P_RG, _P_CA, _P_CG, _P_IW, _P_END = 0, 512, 1024, 1536, 2048, 2560, 2688


def _proj_plain_kernel(x_ref, sc_ref, sh_ref, w_ref, av_ref, avh_ref, rv_ref, rg_ref, z_ref, iw_ref):
    u = (x_ref[0] * (1.0 + sc_ref[0]) + sh_ref[0]).astype(BF16)
    av = _dot(u, w_ref[:, _P_AV:_P_RV])
    av_ref[0] = av
    ones = jnp.ones((av.shape[0], LANES - HEAD_DIM), BF16)
    avb = av.astype(BF16)
    for h in range(N_HEADS):
        avh_ref[0, h] = jnp.concatenate([avb[:, h * HEAD_DIM:(h + 1) * HEAD_DIM], ones], axis=1)
    rv_ref[0] = _dot(u, w_ref[:, _P_RV:_P_RG]).astype(BF16)
    rg_ref[0] = _dot(u, w_ref[:, _P_RG:_P_CA])
    a = _dot(u, w_ref[:, _P_CA:_P_CG])
    g = _dot(u, w_ref[:, _P_CG:_P_IW])
    z_ref[0] = a * jax.nn.sigmoid(g)
    iw_ref[0] = _dot(u, w_ref[:, _P_IW:_P_END])


def _proj_gate_kernel(x_ref, sc_ref, sh_ref, w_ref, gs_ref):
    u = (x_ref[0] * (1.0 + sc_ref[0]) + sh_ref[0]).astype(BF16)
    for c in range(N_BRANCH):
        gs_ref[0, :, c * D_MODEL:(c + 1) * D_MODEL] = jax.nn.sigmoid(
            _dot(u, w_ref[:, c * D_MODEL:(c + 1) * D_MODEL]))


def _mod_spec(mod, tm):
    if mod.shape[1] == 1:
        return pl.BlockSpec((1, 1, D_MODEL), lambda b, i: (b, 0, 0))
    return pl.BlockSpec((1, tm, D_MODEL), lambda b, i: (b, i, 0))


def input_projection(x, sc, sh, cos_t, sin_t, w_rope, w_plain, w_gate, tm):
    B, L, D = x.shape
    grid = (B, L // tm)
    xs = pl.BlockSpec((1, tm, D), lambda b, i: (b, i, 0))
    ms = _mod_spec(sc, tm)
    ts = pl.BlockSpec((tm, LANES), lambda b, i: (i, 0))

    def ws(w):
        return pl.BlockSpec(w.shape, lambda b, i: (0, 0))

    def o(width, dt):
        return jax.ShapeDtypeStruct((B, L, width), dt), pl.BlockSpec((1, tm, width), lambda b, i: (b, i, 0))

    nt = L // tm

    def heads(n, width, dt):
        return jax.ShapeDtypeStruct((B, n, L, width), dt), pl.BlockSpec((1, n, tm, width), lambda b, i: (b, 0, i, 0))

    def chunked_t(rows, dt):
        return jax.ShapeDtypeStruct((B, nt, rows, tm), dt), pl.BlockSpec((1, 1, rows, tm), lambda b, i: (b, i, 0, 0))

    outs = [heads(N_HEADS, HEAD_DIM, BF16), o(ATT_W, F32), chunked_t(ATT_W, BF16), heads(IDX_HEADS, IDX_DIM, BF16),
            o(RET_QK, BF16), o(RET_QK, F32), o(IDX_DIM, F32), chunked_t(IDX_DIM, BF16)]
    aq, ak, akt, iq, rq, rk, ik, ikt = pl.pallas_call(
        _proj_rope_kernel,
        out_shape=[s for s, _ in outs], grid=grid,
        in_specs=[xs, ms, ms, ts, ts, ws(w_rope)],
        out_specs=[s for _, s in outs],
        compiler_params=_params("arbitrary", "arbitrary"), name="proj_rope",
    )(x, sc, sh, cos_t, sin_t, w_rope)

    outs = [o(ATT_W, F32), heads(N_HEADS, LANES, BF16), o(RET_V, BF16), o(RET_V, F32), o(CONV_CH, F32), o(LANES, F32)]
    av, avh, rv, rg, z, iw = pl.pallas_call(
        _proj_plain_kernel,
        out_shape=[s for s, _ in outs], grid=grid,
        in_specs=[xs, ms, ms, ws(w_plain)],
        out_specs=[s for _, s in outs],
        compiler_params=_params("arbitrary", "arbitrary"), name="proj_plain",
    )(x, sc, sh, w_plain)

    gs_shape, gs_spec = o(N_BRANCH * D_MODEL, F32)
    gsig = pl.pallas_call(
        _proj_gate_kernel,
        out_shape=gs_shape, grid=grid,
        in_specs=[xs, ms, ms, ws(w_gate)],
        out_specs=gs_spec,
        compiler_params=_params("arbitrary", "arbitrary"), name="proj_gate",
    )(x, sc, sh, w_gate)
    return dict(aq=aq, ak=ak, akt=akt, iq=iq, rq=rq, rk=rk, ik=ik, ikt=ikt,
                av=av, avh=avh, rv=rv, rg=rg, z=z, iw=iw, gsig=gsig)


def _sort_key(score):
    score = jnp.where(score == 0.0, 0.0, score)
    b = pltpu.bitcast(score, I32)
    return b ^ ((b >> 31) & 0x7FFFFFFF)


def _topk_threshold(count_ge, rows, topk):
    k = float(topk)
    t = jnp.where(count_ge(jnp.zeros((rows, 1), I32)) >= k, 0, INT_MIN).astype(I32)

    def bit_step(i, t):
        cand = t | jnp.left_shift(1, 30 - i).astype(I32)
        return jnp.where(count_ge(cand) >= k, cand, t)

    return lax.fori_loop(0, 31, bit_step, t)


def _tie_bound(count_tie_below, t, need, rows, pos_bits):
    def pos_step(i, j):
        cand = j | jnp.left_shift(1, pos_bits - 1 - i).astype(I32)
        return jnp.where(count_tie_below(t, cand) < need, cand, j)

    return lax.fori_loop(0, pos_bits, pos_step, jnp.zeros((rows, 1), I32))


RADIX_BITS = 3


def _bit_groups(nbits):
    groups, hi = [], nbits
    lead = nbits % RADIX_BITS
    if lead:
        groups.append((nbits - lead, lead))
        hi -= lead
    while hi > 0:
        hi -= RADIX_BITS
        groups.append((hi, RADIX_BITS))
    return groups


def _topk_threshold_radix(count_ge, rows, topk):
    k = float(topk)
    t = jnp.where(count_ge(jnp.zeros((rows, 1), I32)) >= k, 0, INT_MIN).astype(I32)
    for shift, width in _bit_groups(31):
        inc = jnp.zeros((rows, 1), I32)
        for m in range(1, 2 ** width):
            inc = inc + (count_ge(t | (m << shift)) >= k).astype(I32)
        t = t | (inc << shift)
    return t


def _tie_bound_radix(count_tie_below, t, need, rows, pos_bits):
    j = jnp.zeros((rows, 1), I32)
    for shift, width in _bit_groups(pos_bits):
        inc = jnp.zeros((rows, 1), I32)
        for m in range(1, 2 ** width):
            inc = inc + (count_tie_below(t, j | (m << shift)) < need).astype(I32)
        j = j | (inc << shift)
    return j


def _dsa_prompt_kernel(q_ref, iq_ref, iw_ref, kt_ref, vh_ref, ikt_ref, o_ref,
                       key_scr, bias_scr, j_scr, m_scr, acc_scr, *, topk, ck, pos_bits):
    qb = q_ref.shape[2]
    t0 = pl.program_id(1) * qb
    nkc = (t0 + qb + ck - 1) // ck
    qpos = t0 + lax.broadcasted_iota(I32, (qb, 1), 0)
    iw = iw_ref[0]
    iw_cols = [iw[:, h:h + 1] for h in range(IDX_HEADS)]

    def chunk_kpos(c):
        return c * ck + lax.broadcasted_iota(I32, (qb, ck), 1)

    def score_chunk(c, carry):
        ikc = ikt_ref[0, c]
        s = [_dot(iq_ref[0, h], ikc) for h in range(IDX_HEADS)]
        acc = jnp.maximum(s[0], 0.0) * iw_cols[0]
        for h in range(1, IDX_HEADS):
            acc = acc + jnp.maximum(s[h], 0.0) * iw_cols[h]
        key_scr[c] = jnp.where(chunk_kpos(c) <= qpos, _sort_key(acc), INT_MIN)
        return carry

    lax.fori_loop(0, nkc, score_chunk, 0)

    def fold(m):
        s = m[:, :LANES]
        for j in range(1, ck // LANES):
            s = s + m[:, j * LANES:(j + 1) * LANES]
        return s

    def count_ge(cand):
        def body(c, acc):
            return acc + fold((key_scr[c] >= cand).astype(F32))
        return jnp.sum(lax.fori_loop(0, nkc, body, jnp.zeros((qb, LANES), F32)), axis=1, keepdims=True)

    def count_tie_below(t, j):
        def body(c, acc):
            hit = (key_scr[c] == t) & (chunk_kpos(c) < j)
            return acc + fold(hit.astype(F32))
        return jnp.sum(lax.fori_loop(0, nkc, body, jnp.zeros((qb, LANES), F32)), axis=1, keepdims=True)

    t = _topk_threshold(count_ge, qb, topk)
    n_gt = count_ge(t + 1)
    n_tie = count_ge(t) - n_gt
    need = float(topk) - n_gt
    j_scr[...] = jnp.full((qb, 1), 2 ** pos_bits - 1, I32)
    short = jnp.max(jnp.where((n_tie > need) & (t > INT_MIN), 1.0, 0.0))

    @pl.when(short > 0.0)
    def _():
        j_scr[...] = _tie_bound(count_tie_below, t, need, qb, pos_bits)

    j = j_scr[...]

    def bias_chunk(c, carry):
        kpos = chunk_kpos(c)
        key = key_scr[c]
        sel = ((key > t) | ((key == t) & (kpos <= j))) & (kpos <= qpos)
        bias_scr[c] = jnp.where(sel, 0.0, NEG_BIG)
        return carry

    lax.fori_loop(0, nkc, bias_chunk, 0)

    m_scr[...] = jnp.full(m_scr.shape, NEG_BIG, F32)
    acc_scr[...] = jnp.zeros(acc_scr.shape, F32)

    def att_chunk(c, carry):
        off = pl.multiple_of(c * ck, ck)
        bias = bias_scr[c]
        for g in range(0, N_HEADS, HEAD_GROUP):
            hs = range(g, g + HEAD_GROUP)
            s = [_dot(q_ref[0, h], kt_ref[0, c, h * HEAD_DIM:(h + 1) * HEAD_DIM, :]) + bias for h in hs]
            m_prev = [m_scr[h] for h in hs]
            m_new = [jnp.maximum(mp, jnp.max(si, axis=1, keepdims=True)) for mp, si in zip(m_prev, s)]
            p = [jnp.exp(si - jnp.concatenate([mn] * (ck // LANES), axis=1)).astype(BF16) for si, mn in zip(s, m_new)]
            pv = [_dot(pi, vh_ref[0, h, pl.ds(off, ck), :]) for pi, h in zip(p, hs)]
            for i, h in enumerate(hs):
                acc_scr[h] = jnp.exp(m_prev[i] - m_new[i]) * acc_scr[h] + pv[i]
                m_scr[h] = m_new[i]
        return carry

    lax.fori_loop(0, nkc, att_chunk, 0)
    outs = []
    for h in range(N_HEADS):
        acc = acc_scr[h]
        outs.append(acc[:, :HEAD_DIM] / acc[:, HEAD_DIM:2 * HEAD_DIM])
    o_ref[0] = jnp.concatenate(outs, axis=1).astype(o_ref.dtype)


def prompt_sparse_attention(aq, iq, iw, akt, avh, ikt):
    B, _, L, _ = aq.shape
    nc, _, ck = akt.shape[1:]
    topk = min(TOPK_MAX, L // 4)
    qb = Q_BLOCK
    kern = functools.partial(_dsa_prompt_kernel, topk=topk, ck=ck, pos_bits=max(1, (L - 1).bit_length()))
    return pl.pallas_call(
        kern,
        out_shape=jax.ShapeDtypeStruct((B, L, ATT_W), BF16),
        grid=(B, L // qb),
        in_specs=[pl.BlockSpec((1, N_HEADS, qb, HEAD_DIM), lambda b, i: (b, 0, i, 0)),
                  pl.BlockSpec((1, IDX_HEADS, qb, IDX_DIM), lambda b, i: (b, 0, i, 0)),
                  pl.BlockSpec((1, qb, LANES), lambda b, i: (b, i, 0)),
                  pl.BlockSpec((1, nc, ATT_W, ck), lambda b, i: (b, 0, 0, 0)),
                  pl.BlockSpec((1, N_HEADS, L, LANES), lambda b, i: (b, 0, 0, 0)),
                  pl.BlockSpec((1, nc, IDX_DIM, ck), lambda b, i: (b, 0, 0, 0))],
        out_specs=pl.BlockSpec((1, qb, ATT_W), lambda b, i: (b, i, 0)),
        scratch_shapes=[pltpu.VMEM((nc, qb, ck), I32), pltpu.VMEM((nc, qb, ck), F32),
                        pltpu.VMEM((qb, 1), I32),
                        pltpu.VMEM((N_HEADS, qb, LANES), F32),
                        pltpu.VMEM((N_HEADS, qb, LANES), F32)],
        compiler_params=_params("arbitrary", "arbitrary"), name="dsa_prompt",
    )(aq, iq, iw, akt, avh, ikt)


def _dsa_sample_kernel(pt_ref, q_ref, iq_ref, iw_ref, kn_ref, vn_ref, ikn_ref, *rest, n_pages, topk, pos_bits):
    ik_pages = rest[:n_pages]
    k_pages = rest[n_pages:2 * n_pages]
    v_pages = rest[2 * n_pages:3 * n_pages]
    o_ref = rest[3 * n_pages]
    past = n_pages * PAGE_SIZE
    nk = past + LANES
    rows = iq_ref.shape[1]

    iq = iq_ref[0]
    iwc = iw_ref[0]
    parts = []
    for p in range(n_pages):
        s = _dot(iq, ik_pages[p][0, 0].astype(BF16))
        parts.append(jnp.sum(jnp.maximum(s, 0.0) * iwc, axis=0, keepdims=True))
    ikn = ikn_ref[0].astype(BF16).astype(F32)
    s_new = jnp.sum(iq.astype(F32) * ikn, axis=1, keepdims=True)
    s_new = jnp.sum(jnp.maximum(s_new, 0.0) * iwc, axis=0, keepdims=True)
    parts.append(jnp.broadcast_to(s_new, (1, LANES)))
    score = jnp.concatenate(parts, axis=1)
    kpos = lax.broadcasted_iota(I32, (1, nk), 1)
    admissible = kpos <= past
    key = jnp.where(admissible, _sort_key(score), INT_MIN)

    def count_ge(cand):
        return jnp.sum((key >= cand).astype(F32), axis=1, keepdims=True)

    def count_tie_below(t, j):
        return jnp.sum(((key == t) & (kpos < j)).astype(F32), axis=1, keepdims=True)

    t = _topk_threshold_radix(count_ge, 1, topk)
    j = _tie_bound_radix(count_tie_below, t, float(topk) - count_ge(t + 1), 1, pos_bits)
    sel = ((key > t) | ((key == t) & (kpos <= j))) & admissible
    bias = jnp.where(sel, 0.0, NEG_BIG)

    heads = range(N_HEADS)
    qc = [q_ref[0, h] for h in heads]
    sp = []
    for p in range(n_pages):
        sp.append(jnp.concatenate(
            [jnp.sum(k_pages[p][0, 0, h] * qc[h], axis=0, keepdims=True) for h in heads], axis=0))
    s_n = jnp.concatenate([jnp.sum(kn_ref[0, h] * qc[h], axis=0, keepdims=True) for h in heads], axis=0)
    sp.append(jnp.broadcast_to(s_n, (N_HEADS, LANES)))
    s = jnp.concatenate(sp, axis=1) + bias
    m = jnp.max(s, axis=1, keepdims=True)
    p_all = jnp.exp(s - m)
    l = jnp.sum(p_all, axis=1, keepdims=True)
    for h in heads:
        acc = jnp.zeros((HEAD_DIM, PAGE_SIZE), F32)
        for p in range(n_pages):
            acc = acc + v_pages[p][0, 0, h] * p_all[h:h + 1, p * PAGE_SIZE:(p + 1) * PAGE_SIZE]
        o = jnp.sum(acc, axis=1, keepdims=True) + p_all[h:h + 1, past:past + 1] * vn_ref[0, h]
        o_ref[0, h] = o / l[h:h + 1, :]


def sample_sparse_attention(layer, aq, iq, iw, ak, av, ik, cache_k, cache_v, cache_idx_k, page_table):
    DB = aq.shape[0]
    n_pages = page_table.shape[1]
    past = n_pages * PAGE_SIZE
    topk = min(TOPK_MAX, (past + 1) // 4)
    rows = 8
    iq8 = jnp.pad(iq.reshape(DB, IDX_HEADS, IDX_DIM), ((0, 0), (0, rows - IDX_HEADS), (0, 0)))
    iw8 = jnp.pad(iw[:, :IDX_HEADS], ((0, 0), (0, rows - IDX_HEADS))).reshape(DB, rows, 1)

    def row_spec(width):
        return pl.BlockSpec((1, 1, width), lambda b, pt: (b, 0, 0))

    def cols(a):
        return a.reshape(DB, N_HEADS, HEAD_DIM, 1)

    ik_t = jnp.transpose(cache_idx_k, (0, 1, 3, 2))
    ck_t = jnp.transpose(cache_k, (0, 1, 3, 4, 2))
    cv_t = jnp.transpose(cache_v, (0, 1, 3, 4, 2))

    def ik_page_spec(p):
        return pl.BlockSpec((1, 1, IDX_DIM, PAGE_SIZE), lambda b, pt, p=p: (layer, pt[b, p], 0, 0))

    def kv_page_spec(p):
        return pl.BlockSpec((1, 1, N_HEADS, HEAD_DIM, PAGE_SIZE), lambda b, pt, p=p: (layer, pt[b, p], 0, 0, 0))

    col_spec = pl.BlockSpec((1, N_HEADS, HEAD_DIM, 1), lambda b, pt: (b, 0, 0, 0))
    in_specs = [col_spec,
                pl.BlockSpec((1, rows, IDX_DIM), lambda b, pt: (b, 0, 0)),
                pl.BlockSpec((1, rows, 1), lambda b, pt: (b, 0, 0)),
                col_spec, col_spec, row_spec(IDX_DIM)]
    in_specs += [ik_page_spec(p) for p in range(n_pages)]
    in_specs += [kv_page_spec(p) for p in range(n_pages)]
    in_specs += [kv_page_spec(p) for p in range(n_pages)]
    kern = functools.partial(_dsa_sample_kernel, n_pages=n_pages, topk=topk,
                             pos_bits=max(1, (past + LANES - 1).bit_length()))
    out = pl.pallas_call(
        kern,
        out_shape=jax.ShapeDtypeStruct((DB, N_HEADS, HEAD_DIM, 1), F32),
        grid_spec=pltpu.PrefetchScalarGridSpec(
            num_scalar_prefetch=1, grid=(DB,), in_specs=in_specs, out_specs=col_spec),
        compiler_params=_params("arbitrary"), name="dsa_sample",
    )(page_table, cols(aq), iq8, iw8, cols(ak), cols(av),
      ik.reshape(DB, 1, IDX_DIM), *([ik_t] * n_pages), *([ck_t] * n_pages), *([cv_t] * n_pages))
    return out.reshape(DB, ATT_W).astype(BF16)


def _retention_tables(C):
    lg = jnp.log(1.0 - 2.0 ** (-5.0 - jnp.arange(RET_HEADS, dtype=F32)))
    i = jnp.arange(C, dtype=F32)
    diff = i[:, None] - i[None, :]
    causal = diff >= 0
    decay = jnp.where(causal[None], jnp.exp(jnp.where(causal, diff, 0.0)[None] * lg[:, None, None]), 0.0)
    row = jnp.exp((i[None, :] + 1.0) * lg[:, None])
    wk = jnp.exp((C - 1.0 - i)[None, :] * lg[:, None])
    full = jnp.exp(C * lg)
    return (decay, jnp.broadcast_to(row[:, :, None], (RET_HEADS, C, RET_DV)),
            jnp.broadcast_to(wk[:, :, None], (RET_HEADS, C, RET_DK)),
            jnp.broadcast_to(full[:, None, None], (RET_HEADS, 1, RET_DV)))


def _ret_prompt_kernel(q_ref, k_ref, v_ref, dm_ref, rd_ref, kd_ref, gc_ref, o_ref, s_ref, st_scr):
    c = pl.program_id(1)

    @pl.when(c == 0)
    def _():
        st_scr[...] = jnp.zeros(st_scr.shape, F32)

    for h in range(RET_HEADS):
        ks = slice(h * RET_DK, (h + 1) * RET_DK)
        vs = slice(h * RET_DV, (h + 1) * RET_DV)
        qh = q_ref[0, :, ks]
        kh = k_ref[0, :, ks]
        vh = v_ref[0, :, vs]
        st = st_scr[h]
        sc = _dot_nt(qh, kh.astype(BF16)) * dm_ref[h]
        o = _dot(sc.astype(BF16), vh) + _dot(qh, st.astype(BF16)) * rd_ref[h]
        kw = (kh * kd_ref[h]).astype(BF16)
        st_scr[h] = gc_ref[h] * st + _dot_tn(kw, vh)
        o_ref[0, :, vs] = _norm_rows(o)

    @pl.when(c == pl.num_programs(1) - 1)
    def _():
        s_ref[0] = st_scr[...]


def prompt_retention(rq, rk, rv):
    B, L, _ = rq.shape
    C = RET_CHUNK
    dm, rd, kd, gc = _retention_tables(C)

    def full(a):
        return pl.BlockSpec(a.shape, lambda b, c: (0,) * a.ndim)

    return pl.pallas_call(
        _ret_prompt_kernel,
        out_shape=[jax.ShapeDtypeStruct((B, L, RET_V), F32),
                   jax.ShapeDtypeStruct((B, RET_HEADS, RET_DK, RET_DV), F32)],
        grid=(B, L // C),
        in_specs=[pl.BlockSpec((1, C, RET_QK), lambda b, c: (b, c, 0)),
                  pl.BlockSpec((1, C, RET_QK), lambda b, c: (b, c, 0)),
                  pl.BlockSpec((1, C, RET_V), lambda b, c: (b, c, 0)),
                  full(dm), full(rd), full(kd), full(gc)],
        out_specs=[pl.BlockSpec((1, C, RET_V), lambda b, c: (b, c, 0)),
                   pl.BlockSpec((1, RET_HEADS, RET_DK, RET_DV), lambda b, c: (b, 0, 0, 0))],
        scratch_shapes=[pltpu.VMEM((RET_HEADS, RET_DK, RET_DV), F32)],
        compiler_params=_params("arbitrary", "arbitrary"), name="retention_prompt",
    )(rq, rk, rv, dm, rd, kd, gc)


def _ret_sample_kernel(q_ref, k_ref, v_ref, st_ref, rd_ref, gc_ref, o_ref, s_ref):
    G = q_ref.shape[0]
    for g in range(G):
        outs = []
        for h in range(RET_HEADS):
            qc = q_ref[g, h]
            kc = k_ref[g, h]
            vr = v_ref[g, h]
            st = st_ref[0, g, h]
            qk = jnp.sum(qc * kc, axis=0, keepdims=True)
            o = qk * vr + jnp.sum(qc * st, axis=0, keepdims=True) * rd_ref[h]
            s_ref[g, h] = gc_ref[h] * st + kc * vr
            outs.append(_norm_rows(o))
        o_ref[pl.ds(g, 1), :] = jnp.concatenate(outs, axis=1)


def sample_retention(layer, rq, rk, rv, state_ret):
    DB = rq.shape[0]
    G = 8
    _, rd, _, gc = _retention_tables(1)
    qT = rq.reshape(DB, RET_HEADS, RET_DK, 1)
    kT = rk.reshape(DB, RET_HEADS, RET_DK, 1)
    v4 = rv.reshape(DB, RET_HEADS, 1, RET_DV)
    col = pl.BlockSpec((G, RET_HEADS, RET_DK, 1), lambda i: (i, 0, 0, 0))
    return pl.pallas_call(
        _ret_sample_kernel,
        out_shape=[jax.ShapeDtypeStruct((DB, RET_V), F32),
                   jax.ShapeDtypeStruct((DB, RET_HEADS, RET_DK, RET_DV), F32)],
        grid=(DB // G,),
        in_specs=[col, col,
                  pl.BlockSpec((G, RET_HEADS, 1, RET_DV), lambda i: (i, 0, 0, 0)),
                  pl.BlockSpec((1, G, RET_HEADS, RET_DK, RET_DV), lambda i: (layer, i, 0, 0, 0)),
                  pl.BlockSpec(rd.shape, lambda i: (0, 0, 0)),
                  pl.BlockSpec(gc.shape, lambda i: (0, 0, 0))],
        out_specs=[pl.BlockSpec((G, RET_V), lambda i: (i, 0)),
                   pl.BlockSpec((G, RET_HEADS, RET_DK, RET_DV), lambda i: (i, 0, 0, 0))],
        compiler_params=_params("arbitrary"), name="retention_sample",
    )(qT, kT, v4, state_ret, rd, gc)


_HALO = 32


def _conv_post(y, g_ref, b_ref):
    return _silu(_norm_rows(y) * g_ref[...] + b_ref[...])


def _conv_prompt_kernel(z_ref, dw_ref, db_ref, g_ref, b_ref, o_ref, zp_scr, *, tl, rs):
    @pl.when(pl.program_id(1) == 0)
    def _():
        zp_scr[0:_HALO, :] = jnp.zeros((_HALO, CONV_CH), F32)

    zp_scr[_HALO:_HALO + tl, :] = z_ref[0]
    first = _HALO - (CONV_W - 1)
    for r in range(tl // rs):
        acc = jnp.broadcast_to(db_ref[...], (rs, CONV_CH))
        for k in range(CONV_W):
            lo = first + k + r * rs
            acc = acc + zp_scr[lo:lo + rs, :] * dw_ref[k:k + 1, :]
        o_ref[0, r * rs:(r + 1) * rs, :] = _conv_post(acc, g_ref, b_ref).astype(o_ref.dtype)
    zp_scr[0:_HALO, :] = zp_scr[tl:tl + _HALO, :]


def prompt_conv(z, dw, db, ln_g, ln_b):
    B, L, _ = z.shape
    tl = min(512, L)
    rs = min(128, tl)
    vec = pl.BlockSpec((1, CONV_CH), lambda b, i: (0, 0))
    return pl.pallas_call(
        functools.partial(_conv_prompt_kernel, tl=tl, rs=rs),
        out_shape=jax.ShapeDtypeStruct((B, L, CONV_CH), BF16),
        grid=(B, L // tl),
        in_specs=[pl.BlockSpec((1, tl, CONV_CH), lambda b, i: (b, i, 0)),
                  pl.BlockSpec((CONV_W, CONV_CH), lambda b, i: (0, 0)), vec, vec, vec],
        out_specs=pl.BlockSpec((1, tl, CONV_CH), lambda b, i: (b, i, 0)),
        scratch_shapes=[pltpu.VMEM((_HALO + tl, CONV_CH), F32)],
        compiler_params=_params("arbitrary", "arbitrary"), name="conv_prompt",
    )(z, dw, db.reshape(1, -1), ln_g.reshape(1, -1), ln_b.reshape(1, -1))


def _conv_sample_kernel(z_ref, buf_ref, dw_ref, db_ref, g_ref, b_ref, o_ref, nb_ref):
    G = z_ref.shape[0]
    w_past = dw_ref[0:CONV_W - 1, :]
    w_last = dw_ref[CONV_W - 1:CONV_W, :]
    rows = []
    for g in range(G):
        buf = buf_ref[0, g]
        zr = z_ref[pl.ds(g, 1), :]
        rows.append(jnp.sum(buf * w_past, axis=0, keepdims=True) + zr * w_last + db_ref[...])
        nb_ref[g, 0:CONV_W - 2, :] = buf_ref[0, g, 1:CONV_W - 1, :]
        nb_ref[g, CONV_W - 2:CONV_W - 1, :] = zr
    o_ref[...] = _conv_post(jnp.concatenate(rows, axis=0), g_ref, b_ref).astype(o_ref.dtype)


def sample_conv(layer, z, state_conv, dw, db, ln_g, ln_b):
    DB = z.shape[0]
    G = 8
    vec = pl.BlockSpec((1, CONV_CH), lambda i: (0, 0))
    return pl.pallas_call(
        _conv_sample_kernel,
        out_shape=[jax.ShapeDtypeStruct((DB, CONV_CH), BF16),
                   jax.ShapeDtypeStruct((DB, CONV_W - 1, CONV_CH), F32)],
        grid=(DB // G,),
        in_specs=[pl.BlockSpec((G, CONV_CH), lambda i: (i, 0)),
                  pl.BlockSpec((1, G, CONV_W - 1, CONV_CH), lambda i: (layer, i, 0, 0)),
                  pl.BlockSpec((CONV_W, CONV_CH), lambda i: (0, 0)), vec, vec, vec],
        out_specs=[pl.BlockSpec((G, CONV_CH), lambda i: (i, 0)),
                   pl.BlockSpec((G, CONV_W - 1, CONV_CH), lambda i: (i, 0, 0))],
        compiler_params=_params("arbitrary"), name="conv_sample",
    )(z, state_conv, dw, db.reshape(1, -1), ln_g.reshape(1, -1), ln_b.reshape(1, -1))


def _merge_kernel(x_ref, oret_ref, rg_ref, act_ref, att_ref, gs_ref, g1_ref, sc2_ref, sh2_ref,
                  wr_ref, wc_ref, bc_ref, wa_ref, wo_ref, lg_ref, lb_ref, *rest):
    x1_ref, h_ref = rest[-2], rest[-1]
    D = D_MODEL
    y_ret = _dot((_silu(rg_ref[0]) * oret_ref[0]).astype(BF16), wr_ref[...])
    y_conv = _dot(act_ref[0], wc_ref[...]) + bc_ref[...]
    y_att = _dot(att_ref[0], wa_ref[...])
    m = gs_ref[0, :, 0:D] * y_ret + gs_ref[0, :, D:2 * D] * y_conv + gs_ref[0, :, 2 * D:3 * D] * y_att
    mix = _dot(m.astype(BF16), wo_ref[...])
    x1 = _norm_rows(ALPHA * x_ref[0] + g1_ref[0] * mix) * lg_ref[...] + lb_ref[...]
    x1_ref[0] = x1
    h_ref[...] = x1 * (1.0 + sc2_ref[0]) + sh2_ref[0]


def merge_and_norm(x, oret, rg, act, att, gsig, g1, sc2, sh2, w_ret, w_conv, b_conv, w_att, w_o, ln_g, ln_b,
                   h_all, n_all, row0, tm):
    B, L, D = x.shape
    nl = L // tm
    blk0 = row0 // tm

    def rows(width):
        return pl.BlockSpec((1, tm, width), lambda b, i: (b, i, 0))

    def full(a):
        return pl.BlockSpec(a.shape, lambda b, i: (0,) * a.ndim)

    vec = pl.BlockSpec((1, D), lambda b, i: (0, 0))
    h_spec = pl.BlockSpec((tm, D), lambda b, i: (blk0 + b * nl + i, 0))
    ms = _mod_spec(g1, tm)
    args = [x, oret, rg, act, att, gsig, g1, sc2, sh2, w_ret, w_conv, b_conv.reshape(1, D), w_att, w_o,
            ln_g.reshape(1, D), ln_b.reshape(1, D)]
    in_specs = [rows(D), rows(RET_V), rows(RET_V), rows(CONV_CH), rows(ATT_W), rows(N_BRANCH * D), ms, ms, ms,
                full(w_ret), full(w_conv), vec, full(w_att), full(w_o), vec, vec]
    aliases = {}
    if h_all is not None:
        args.append(h_all)
        in_specs.append(pl.BlockSpec(memory_space=pl.ANY))
        aliases = {len(args) - 1: 1}
    return pl.pallas_call(
        _merge_kernel,
        out_shape=[jax.ShapeDtypeStruct((B, L, D), F32), jax.ShapeDtypeStruct((n_all, D), F32)],
        grid=(B, nl), in_specs=in_specs,
        out_specs=[rows(D), h_spec],
        input_output_aliases=aliases,
        compiler_params=_params("arbitrary", "arbitrary"), name="merge_norm",
    )(*args)


def _router_kernel(h_ref, whi_ref, wlo_ref, b_ref, e_ref, g_ref):
    h = h_ref[...]
    h_hi = h.astype(BF16)
    h_lo = (h - h_hi.astype(F32)).astype(BF16)
    logits = (_dot(h_hi, whi_ref[...]) + (_dot(h_hi, wlo_ref[...]) + _dot(h_lo, whi_ref[...]))) + b_ref[...]
    lane = lax.broadcasted_iota(I32, logits.shape, 1).astype(F32)
    vals = logits
    e_out = jnp.zeros(logits.shape, F32)
    top = []
    for k in range(MOE_TOP_K):
        m = jnp.max(vals, axis=1, keepdims=True)
        idx = jnp.min(jnp.where(vals == m, lane, float(LANES)), axis=1, keepdims=True)
        e_out = jnp.where(lane == float(k), idx, e_out)
        vals = jnp.where(lane == idx, -jnp.inf, vals)
        top.append(m)
    ex = [jnp.exp(t - top[0]) for t in top]
    den = ex[0] + ex[1] + ex[2] + ex[3]
    g_out = jnp.zeros(logits.shape, F32)
    for k in range(MOE_TOP_K):
        g_out = jnp.where(lane == float(k), ex[k] / den, g_out)
    e_ref[...] = e_out.astype(I32)
    g_ref[...] = g_out


def router_top4(h_all, router_w, router_b, tm):
    N, D = h_all.shape
    wp = jnp.zeros((D, LANES), F32).at[:, :N_EXPERTS].set(router_w)
    w_hi = wp.astype(BF16)
    w_lo = (wp - w_hi.astype(F32)).astype(BF16)
    bp = jnp.full((1, LANES), NEG_BIG, F32).at[0, :N_EXPERTS].set(router_b)
    e, g = pl.pallas_call(
        _router_kernel,
        out_shape=[jax.ShapeDtypeStruct((N, LANES), I32), jax.ShapeDtypeStruct((N, LANES), F32)],
        grid=(N // tm,),
        in_specs=[pl.BlockSpec((tm, D), lambda i: (i, 0)),
                  pl.BlockSpec((D, LANES), lambda i: (0, 0)),
                  pl.BlockSpec((D, LANES), lambda i: (0, 0)),
                  pl.BlockSpec((1, LANES), lambda i: (0, 0))],
        out_specs=[pl.BlockSpec((tm, LANES), lambda i: (i, 0)), pl.BlockSpec((tm, LANES), lambda i: (i, 0))],
        compiler_params=_params("arbitrary"), name="router_top4",
    )(h_all, w_hi, w_lo, bp)
    return e[:, :MOE_TOP_K], g[:, :MOE_TOP_K]


def _moe_kernel(blk_e_ref, nreal_ref, idx_hbm, h_hbm, g_ref, wgu_ref, bgu_ref, wd_ref, bd_ref, y_hbm,
                idx_smem, xbuf, obuf, hid_scr, wgu_bf, wd_bf, sem_idx, sem_g, sem_s, *, bm, nc):
    i = pl.program_id(0)
    n = pl.num_programs(0)

    def idx_copy(c):
        slot = c % 3
        return pltpu.make_async_copy(idx_hbm.at[c], idx_smem.at[pl.ds(slot * 2 * bm, 2 * bm)], sem_idx.at[slot])

    def issue_gather(c):
        base = (c % 3) * 2 * bm
        buf = c % 2

        @pl.when(nreal_ref[c] > 0)
        def _():
            dst = xbuf.at[buf]
            for r in range(bm):
                tok = idx_smem[base + r]
                pltpu.make_async_copy(h_hbm.at[pl.ds(tok, 1)], dst.at[pl.ds(r, 1)], sem_g.at[buf]).start()

    def wait_gather(c):
        buf = c % 2

        @pl.when(nreal_ref[c] > 0)
        def _():
            pltpu.make_async_copy(h_hbm.at[pl.ds(0, bm)], xbuf.at[buf], sem_g.at[buf]).wait()

    def wait_scatter(c):
        @pl.when(nreal_ref[c] > 0)
        def _():
            pltpu.make_async_copy(obuf, y_hbm.at[pl.ds(0, bm)], sem_s).wait()

    @pl.when(i == 0)
    def _():
        idx_copy(0).start()
        idx_copy(0).wait()
        issue_gather(0)

        @pl.when(n > 1)
        def _():
            idx_copy(1).start()

    @pl.when(i + 1 < n)
    def _():
        idx_copy(i + 1).wait()
        issue_gather(i + 1)

        @pl.when(i + 2 < n)
        def _():
            idx_copy(i + 2).start()

    nr = nreal_ref[i]
    buf = i % 2
    wait_gather(i)

    @pl.when((i == 0) | (blk_e_ref[i] != blk_e_ref[jnp.maximum(i - 1, 0)]))
    def _():
        wgu_bf[...] = wgu_ref[0].astype(BF16)
        wd_bf[...] = wd_ref[0].astype(BF16)

    @pl.when(i > 0)
    def _():
        wait_scatter(jnp.maximum(i - 1, 0))

    @pl.when(nr > 0)
    def _():
        x = xbuf[buf].astype(BF16)
        for j in range(D_EXPERT // nc):
            lo = j * nc
            a = _dot(x, wgu_bf[:, lo:lo + nc]) + bgu_ref[0, :, lo:lo + nc]
            lin = _dot(x, wgu_bf[:, D_EXPERT + lo:D_EXPERT + lo + nc]) + bgu_ref[0, :, D_EXPERT + lo:D_EXPERT + lo + nc]
            a = jnp.minimum(a, SWIGLU_LIMIT)
            lin = jnp.clip(lin, -SWIGLU_LIMIT, SWIGLU_LIMIT)
            hid_scr[:, lo:lo + nc] = (a * jax.nn.sigmoid(SWIGLU_ALPHA * a) * (lin + 1.0)).astype(BF16)
        obuf[...] = (_dot(hid_scr[...], wd_bf[...]) + bd_ref[0]) * g_ref[0]

        base = (i % 3) * 2 * bm + bm
        for r in range(bm):
            dst = idx_smem[base + r]
            pltpu.make_async_copy(obuf.at[pl.ds(r, 1)], y_hbm.at[pl.ds(dst, 1)], sem_s).start()

    @pl.when(i == n - 1)
    def _():
        wait_scatter(i)


def moe_dispatch(top_e, gate, bm):
    n_tok = top_e.shape[0]
    n_assign = n_tok * MOE_TOP_K
    flat_e = top_e.reshape(-1)
    order = jnp.argsort(flat_e).astype(I32)
    se = flat_e[order]
    stok = order // MOE_TOP_K
    sk = order % MOE_TOP_K
    sg = gate.reshape(-1)[order]
    counts = jnp.sum((flat_e[:, None] == jnp.arange(N_EXPERTS, dtype=I32)[None, :]).astype(I32), axis=0)
    starts = jnp.cumsum(counts) - counts
    rank = jnp.arange(n_assign, dtype=I32) - starts[se]
    pcounts = (counts + bm - 1) // bm * bm
    pends = jnp.cumsum(pcounts)
    pstarts = pends - pcounts
    ppos = pstarts[se] + rank
    n_chunks = -(-n_assign // bm) + N_EXPERTS
    n_slots = n_chunks * bm
    spare = n_assign + jnp.arange(n_slots, dtype=I32) % bm
    init = jnp.stack([spare, jnp.zeros((n_slots,), I32)], axis=1)
    vals = jnp.stack([sk * n_tok + stok, lax.bitcast_convert_type(sg, I32)], axis=1)
    packed = init.at[ppos].set(vals)
    buf_dst = packed[:, 0]
    buf_tok = jnp.where(buf_dst < n_assign, buf_dst % n_tok, 0)
    buf_g = lax.bitcast_convert_type(packed[:, 1], F32)
    chunk0 = jnp.arange(n_chunks, dtype=I32) * bm
    blk_e = jnp.minimum(jnp.sum((chunk0[:, None] >= pends[None, :]).astype(I32), axis=1), N_EXPERTS - 1)
    nreal = jnp.clip(pstarts[blk_e] + counts[blk_e] - chunk0, 0, bm).astype(I32)
    idx = jnp.concatenate([buf_tok.reshape(n_chunks, bm), buf_dst.reshape(n_chunks, bm)], axis=1)
    return blk_e, nreal, idx, buf_g.reshape(n_chunks, bm, 1)


def moe_experts(h_all, top_e, gate, layer, exp_w_gu, exp_b_gu, exp_w_down, exp_b_down):
    blk_e, nreal, idx, gcol = moe_dispatch(top_e, gate, MOE_BM)
    return moe_grouped_matmul(blk_e, nreal, idx, gcol, h_all, layer, exp_w_gu, exp_b_gu, exp_w_down, exp_b_down)


def moe_grouped_matmul(blk_e, nreal, idx, gcol, h_all, layer, exp_w_gu, exp_b_gu, exp_w_down, exp_b_down):
    N, D = h_all.shape
    bm = MOE_BM
    nc = 256
    n_chunks = idx.shape[0]
    b_gu = exp_b_gu.reshape(DEPTH, N_EXPERTS, 1, 2 * D_EXPERT)
    b_dn = exp_b_down.reshape(DEPTH, N_EXPERTS, 1, D)
    grid_spec = pltpu.PrefetchScalarGridSpec(
        num_scalar_prefetch=2, grid=(n_chunks,),
        in_specs=[pl.BlockSpec(memory_space=pl.ANY),
                  pl.BlockSpec(memory_space=pl.ANY),
                  pl.BlockSpec((1, bm, 1), lambda i, be, nr: (i, 0, 0)),
                  pl.BlockSpec((1, 1, D, 2 * D_EXPERT), lambda i, be, nr: (layer, be[i], 0, 0)),
                  pl.BlockSpec((1, 1, 1, 2 * D_EXPERT), lambda i, be, nr: (layer, be[i], 0, 0)),
                  pl.BlockSpec((1, 1, D_EXPERT, D), lambda i, be, nr: (layer, be[i], 0, 0)),
                  pl.BlockSpec((1, 1, 1, D), lambda i, be, nr: (layer, be[i], 0, 0))],
        out_specs=pl.BlockSpec(memory_space=pl.ANY),
        scratch_shapes=[pltpu.SMEM((3 * 2 * bm,), I32),
                        pltpu.VMEM((2, bm, D), F32),
                        pltpu.VMEM((bm, D), F32),
                        pltpu.VMEM((bm, D_EXPERT), BF16),
                        pltpu.VMEM((D, 2 * D_EXPERT), BF16),
                        pltpu.VMEM((D_EXPERT, D), BF16),
                        pltpu.SemaphoreType.DMA((3,)),
                        pltpu.SemaphoreType.DMA((2,)),
                        pltpu.SemaphoreType.DMA(())])
    y = pl.pallas_call(
        functools.partial(_moe_squeeze_kernel, bm=bm, nc=nc),
        out_shape=jax.ShapeDtypeStruct((MOE_TOP_K * N + bm, D), F32),
        grid_spec=grid_spec,
        compiler_params=_params("arbitrary"), name="moe_experts",
    )(blk_e, nreal, idx, h_all, gcol, exp_w_gu, b_gu, exp_w_down, b_dn)
    return y


def _moe_squeeze_kernel(blk_e_ref, nreal_ref, idx_hbm, h_hbm, g_ref, wgu_ref, bgu_ref, wd_ref, bd_ref, y_hbm,
                        *scratch, bm, nc):
    _moe_kernel(blk_e_ref, nreal_ref, idx_hbm, h_hbm, g_ref, wgu_ref.at[0], bgu_ref.at[0], wd_ref.at[0],
                bd_ref.at[0], y_hbm, *scratch, bm=bm, nc=nc)


def _combine_kernel(x_ref, y0_ref, y1_ref, y2_ref, y3_ref, g2_ref, lg_ref, lb_ref, o_ref):
    ff = (y0_ref[...] + y1_ref[...]) + (y2_ref[...] + y3_ref[...])
    o_ref[0] = _norm_rows(ALPHA * x_ref[0] + g2_ref[0] * ff) * lg_ref[...] + lb_ref[...]


def combine_and_norm(x1, y, n_all, g2, ln_g, ln_b, row0, tm):
    B, L, D = x1.shape
    nl = L // tm
    blk0 = row0 // tm
    per_k = n_all // tm
    vec = pl.BlockSpec((1, D), lambda b, i: (0, 0))

    def y_spec(k):
        return pl.BlockSpec((tm, D), lambda b, i, k=k: (k * per_k + blk0 + b * nl + i, 0))

    return pl.pallas_call(
        _combine_kernel,
        out_shape=jax.ShapeDtypeStruct((B, L, D), F32),
        grid=(B, nl),
        in_specs=[pl.BlockSpec((1, tm, D), lambda b, i: (b, i, 0))] + [y_spec(k) for k in range(MOE_TOP_K)]
        + [_mod_spec(g2, tm), vec, vec],
        out_specs=pl.BlockSpec((1, tm, D), lambda b, i: (b, i, 0)),
        compiler_params=_params("arbitrary", "arbitrary"), name="combine_norm",
    )(x1, y, y, y, y, g2, ln_g.reshape(1, D), ln_b.reshape(1, D))


def _rope_tables(pos):
    half = HEAD_DIM // 2
    inv = 1.0 / (ROPE_THETA ** (jnp.arange(half, dtype=F32) / half))
    ang = pos.astype(F32)[:, None] * inv[None, :]
    cos, sin = jnp.cos(ang), jnp.sin(ang)
    reps = LANES // HEAD_DIM
    return jnp.tile(jnp.concatenate([cos, cos], axis=1), (1, reps)), jnp.tile(jnp.concatenate([-sin, sin], axis=1), (1, reps))


def _split_w_in(w):
    o = IN_OFFSETS
    aq, ak, av, iq, ik, iw, rq, rk, rv, rg, cglu, gl = [w[:, o[i]:o[i + 1]] for i in range(len(IN_SPLITS))]
    D = w.shape[0]
    pad_ik = jnp.zeros((D, _R_END - _R_IK - IDX_DIM), F32)
    pad_iw = jnp.zeros((D, _P_END - _P_IW - IDX_HEADS), F32)
    w_rope = jnp.concatenate([aq * (HEAD_DIM ** -0.5), ak, iq, rq, rk * (RET_DK ** -0.5), ik, pad_ik], axis=1)
    w_plain = jnp.concatenate([av, rv, rg, cglu, iw * ((IDX_DIM ** -0.5) * (IDX_HEADS ** -0.5)), pad_iw], axis=1)
    return w_rope.astype(BF16), w_plain.astype(BF16), gl.astype(BF16)


def kernel(x_prompt, x_sample, c_prompt, c_sample, cache_k, cache_v, cache_idx_k, state_ret, state_conv, page_table, ada_w, ada_b, w_in, w_ret_out, conv_dw, conv_db, conv_ln_g, conv_ln_b, w_conv_out, b_conv_out, w_att_out, w_o, ln1_g, ln1_b, router_w, router_b, exp_w_gu, exp_b_gu, exp_w_down, exp_b_down, ln2_g, ln2_b):
    B, L, D = x_prompt.shape
    DB = x_sample.shape[0]
    n_p = B * L
    n_all = n_p + DB
    past = page_table.shape[1] * PAGE_SIZE

    r_pad = -(B + DB) % 8
    c_all = jnp.concatenate([c_prompt, c_sample, jnp.zeros((r_pad, D), F32)], axis=0)
    mod = ada_modulation(c_all, ada_w, ada_b)

    cos_p, sin_p = _rope_tables(jnp.arange(L, dtype=I32))
    cos_s, sin_s = _rope_tables(jnp.full((DB,), past, I32))

    xp = x_prompt
    xs = x_sample.reshape(1, DB, D)
    new_p = ([], [], [], [], [])
    new_s = ([], [], [], [], [])
    tm_p = min(PROJ_TM, L)
    tm_m = min(256, L)
    tm_r = math.gcd(n_all, 384)
    tm_c = math.gcd(math.gcd(L, DB), 128)

    for l in range(DEPTH):
        mp = [mod[l, :B, i * D:(i + 1) * D].reshape(B, 1, D) for i in range(6)]
        ms = [mod[l, B:B + DB, i * D:(i + 1) * D].reshape(1, DB, D) for i in range(6)]
        w_rope, w_plain, w_gate = _split_w_in(w_in[l])
        wr, wc, wa, wo = (w_ret_out[l].astype(BF16), w_conv_out[l].astype(BF16), w_att_out[l].astype(BF16),
                          w_o[l].astype(BF16))

        pp = input_projection(xp, mp[1], mp[0], cos_p, sin_p, w_rope, w_plain, w_gate, tm_p)
        ps = input_projection(xs, ms[1], ms[0], cos_s, sin_s, w_rope, w_plain, w_gate, DB)

        att_p = prompt_sparse_attention(pp['aq'], pp['iq'], pp['iw'], pp['akt'], pp['avh'], pp['ikt'])
        oret_p, sret_p = prompt_retention(pp['rq'], pp['rk'], pp['rv'])
        act_p = prompt_conv(pp['z'], conv_dw[l], conv_db[l], conv_ln_g[l], conv_ln_b[l])

        aq_s = jnp.transpose(ps['aq'][0], (1, 0, 2)).reshape(DB, ATT_W).astype(F32)
        iq_s = jnp.transpose(ps['iq'][0], (1, 0, 2)).reshape(DB, IDX_HEADS * IDX_DIM)
        att_s = sample_sparse_attention(l, aq_s, iq_s, ps['iw'][0], ps['ak'][0],
                                        ps['av'][0], ps['ik'][0], cache_k, cache_v, cache_idx_k, page_table)
        oret_s, sret_s = sample_retention(l, ps['rq'][0].astype(F32), ps['rk'][0], ps['rv'][0].astype(F32), state_ret)
        act_s, conv_s = sample_conv(l, ps['z'][0], state_conv, conv_dw[l], conv_db[l], conv_ln_g[l], conv_ln_b[l])

        x1p, h_all = merge_and_norm(xp, oret_p, pp['rg'], act_p, att_p, pp['gsig'], mp[2], mp[4], mp[3],
                                    wr, wc, b_conv_out[l], wa, wo, ln1_g[l], ln1_b[l], None, n_all, 0, tm_m)
        x1s, h_all = merge_and_norm(xs, oret_s.reshape(1, DB, -1), ps['rg'], act_s.reshape(1, DB, -1),
                                    att_s.reshape(1, DB, -1), ps['gsig'], ms[2], ms[4], ms[3],
                                    wr, wc, b_conv_out[l], wa, wo, ln1_g[l], ln1_b[l], h_all, n_all, n_p, DB)

        top_e, gate = router_top4(h_all, router_w[l], router_b[l], tm_r)
        y4 = moe_experts(h_all, top_e, gate, l, exp_w_gu, exp_b_gu, exp_w_down, exp_b_down)
        xp = combine_and_norm(x1p, y4, n_all, mp[5], ln2_g[l], ln2_b[l], 0, tm_c)
        xs = combine_and_norm(x1s, y4, n_all, ms[5], ln2_g[l], ln2_b[l], n_p, tm_c)

        for lst, a in zip(new_p, (pp['ak'].reshape(B, L, N_HEADS, HEAD_DIM), pp['av'].reshape(B, L, N_HEADS, HEAD_DIM),
                                  pp['ik'], sret_p, pp['z'][:, L - (CONV_W - 1):, :])):
            lst.append(a)
        for lst, a in zip(new_s, (ps['ak'].reshape(DB, 1, N_HEADS, HEAD_DIM), ps['av'].reshape(DB, 1, N_HEADS, HEAD_DIM),
                                  ps['ik'].reshape(DB, 1, IDX_DIM), sret_s, conv_s)):
            lst.append(a)

    nk_p, nv_p, nik_p, nr_p, nc_p = [jnp.stack(a) for a in new_p]
    nk_s, nv_s, nik_s, nr_s, nc_s = [jnp.stack(a) for a in new_s]
    return (xp, xs.reshape(DB, 1, D), nk_p, nv_p, nik_p, nr_p, nc_p, nk_s, nv_s, nik_s, nr_s, nc_s)
```
